```python
import math
import jax
import jax.numpy as jnp
from jax import lax
import numpy as np

D_MODEL = 4096
BATCH = 16
SEQ = 256
DEPTH = 4
DEC_BATCH = 8
DEC_SEQ = 1024
PAST_LEN = 256

GRID_W = 64
HEAD_DIM = 128
ROPE_BASE = 10000.0
Q_BLOCK = 128
NORM_EPS = 1e-6
CONV_WIDTH = 5
GROUP_WIDTH = D_MODEL // 2
N_MOD = 6
GQA_HEADS = GROUP_WIDTH // HEAD_DIM
GQA_KV_HEADS = GQA_HEADS // 4
SSD_INNER = GROUP_WIDTH
SSD_HEAD_DIM = 64
SSD_HEADS = SSD_INNER // SSD_HEAD_DIM
SSD_GROUPS = 4
SSD_STATE = 128
SSD_CHUNK = 128
SSD_CONV_CH = SSD_INNER + 2 * SSD_GROUPS * SSD_STATE
GDN_HEADS = GROUP_WIDTH // HEAD_DIM
GDN_WIDTH = GDN_HEADS * HEAD_DIM
GDN_CHUNK = 64
MLA_HEADS = GROUP_WIDTH // HEAD_DIM
MLA_Q_LORA = D_MODEL // 4
MLA_KV_LORA = 512
MLA_NOPE = 128
MLA_ROPE = 64
MLA_V = 128
N_EXPERTS = 32
TOP_K = 4
EXPERT_FF = 1024
SWIGLU_LIMIT = 7.0
SWIGLU_ALPHA = 1.702
MOE_BLOCK = 128
EVEN_SPLITS = (GQA_HEADS * HEAD_DIM, GQA_KV_HEADS * HEAD_DIM, GQA_KV_HEADS * HEAD_DIM, SSD_INNER, SSD_CONV_CH, 2 * SSD_HEADS)
ODD_SPLITS = (3 * GDN_WIDTH, GDN_WIDTH, 4 * GDN_HEADS, MLA_Q_LORA, MLA_KV_LORA, MLA_ROPE)
EVEN_IN = sum(EVEN_SPLITS)
ODD_IN = sum(ODD_SPLITS)
EVEN_OUT_IN = GQA_HEADS * HEAD_DIM + SSD_INNER
ODD_OUT_IN = GDN_WIDTH + MLA_HEADS * MLA_V

kernel_name = 'hybrid_diffusion_prefix_trunk_step'


def split_sizes(x, sizes):
    return jnp.split(x, np.cumsum(sizes)[:-1].tolist(), axis=-1)


def flip(x):
    return jnp.flip(x, axis=1)


def rms_norm(x, w):
    xf = x.astype(jnp.float32)
    return (xf * lax.rsqrt(jnp.mean(xf * xf, axis=-1, keepdims=True) + NORM_EPS)).astype(x.dtype) * w


def l2_normalize(x):
    xf = x.astype(jnp.float32)
    return (xf * lax.rsqrt(jnp.sum(xf * xf, axis=-1, keepdims=True) + NORM_EPS)).astype(x.dtype)


def rope_1d(x, pos):
    half = x.shape[-1] // 2
    inv_freq = ROPE_BASE ** (-jnp.arange(half, dtype=jnp.float32) / half)
    ang = pos.astype(jnp.float32)[:, None] * inv_freq
    cos = jnp.cos(ang)[:, None, :].astype(x.dtype)
    sin = jnp.sin(ang)[:, None, :].astype(x.dtype)
    x1, x2 = x[..., :half], x[..., half:]
    return jnp.concatenate([x1 * cos - x2 * sin, x2 * cos + x1 * sin], axis=-1)


def rope_2d(x, n_rows):
    t = jnp.arange(n_rows * GRID_W)
    d_axis = x.shape[-1] // 2
    return jnp.concatenate([rope_1d(x[..., :d_axis], t // GRID_W), rope_1d(x[..., d_axis:], t % GRID_W)], axis=-1)


def short_conv(x, w):
    pad = (w.shape[0] - 1) // 2
    return lax.conv_general_dilated(x, w[:, None, :], window_strides=(1,), padding=[(pad, pad)],
                                    dimension_numbers=('NWC', 'WIO', 'NWC'), feature_group_count=x.shape[-1])


def block_attention(q, k, v):
    b, sq = q.shape[:2]
    nb = sq // Q_BLOCK
    scale = q.shape[-1] ** -0.5
    qb = jnp.swapaxes(q.reshape(b, nb, Q_BLOCK, *q.shape[2:]), 0, 1)

    def one_block(qblk):
        s = jnp.einsum('bqkgd,bskd->bkgqs', qblk, k).astype(jnp.float32) * scale
        p = jax.nn.softmax(s, axis=-1).astype(v.dtype)
        return jnp.einsum('bkgqs,bske->bqkge', p, v)

    o = lax.map(one_block, qb)
    return jnp.swapaxes(o, 0, 1).reshape(b, sq, *o.shape[3:])


def ssd_scan(x, dt, a, bm, cm, h0):
    b, s, nh, p = x.shape
    ng, n = bm.shape[-2:]
    L = SSD_CHUNK
    nc = s // L
    rep = nh // ng
    bc = jnp.repeat(bm, rep, axis=2).reshape(b, nc, L, nh, n)
    cc = jnp.repeat(cm, rep, axis=2).reshape(b, nc, L, nh, n)
    xdt = (x * dt[..., None]).reshape(b, nc, L, nh, p)
    cs = jnp.cumsum((dt * a).reshape(b, nc, L, nh), axis=2)
    causal = jnp.tril(jnp.ones((L, L), bool))[:, :, None]
    decay = jnp.exp(jnp.where(causal, cs[:, :, :, None, :] - cs[:, :, None, :, :], -jnp.inf))
    y_intra = jnp.einsum('bcijh,bcjhp->bcihp', jnp.einsum('bcihn,bcjhn->bcijh', cc, bc) * decay, xdt)
    chunk_states = jnp.einsum('bclhn,bclhp->bchpn', bc * jnp.exp(cs[:, :, -1:, :] - cs)[..., None], xdt)
    chunk_decay = jnp.exp(cs[:, :, -1, :])

    def step(state, inp):
        st, dec = inp
        return state * dec[:, :, None, None] + st, state

    h_last, h_prev = lax.scan(step, h0.astype(chunk_states.dtype),
                              (jnp.swapaxes(chunk_states, 0, 1), jnp.swapaxes(chunk_decay, 0, 1)))
    h_prev = jnp.swapaxes(h_prev, 0, 1)
    y_inter = jnp.einsum('bclhn,bchpn->bclhp', cc * jnp.exp(cs)[..., None], h_prev)
    return (y_intra + y_inter).reshape(b, s, nh, p), h_last


def gated_delta_chunked(q, k, v, g, beta, s0):
    b, s, nh, dk = q.shape
    dv = v.shape[-1]
    L = GDN_CHUNK
    nc = s // L

    def chunks(t):
        return jnp.moveaxis(t.reshape(b, nc, L, *t.shape[2:]), 2, 3)

    qc, kc, vc = chunks(q * dk ** -0.5), chunks(k), chunks(v)
    gcum = jnp.cumsum(chunks(g), axis=-1)
    bc = chunks(beta)
    incl = jnp.tril(jnp.ones((L, L), bool))
    strict = jnp.tril(jnp.ones((L, L), bool), -1)
    gamma = jnp.exp(jnp.where(incl, gcum[..., :, None] - gcum[..., None, :], -jnp.inf))
    a_mat = jnp.where(strict, jnp.einsum('bnhik,bnhjk->bnhij', kc, kc) * gamma * bc[..., :, None], 0.0)
    m = (a_mat + jnp.eye(L, dtype=a_mat.dtype)).astype(jnp.float32)

    def solve(rhs):
        return lax.linalg.triangular_solve(m, rhs.astype(jnp.float32), left_side=True, lower=True,
                                           unit_diagonal=True).astype(q.dtype)

    u = solve(vc * bc[..., None])
    w = solve(kc * (bc * jnp.exp(gcum))[..., None])
    qk = jnp.einsum('bnhik,bnhjk->bnhij', qc, kc) * gamma
    q_dec = qc * jnp.exp(gcum)[..., None]
    k_dec = kc * jnp.exp(gcum[..., -1:] - gcum)[..., None]
    end_dec = jnp.exp(gcum[..., -1])

    def step(state, inp):
        u_c, w_c, qk_c, qd_c, kd_c, ed_c = inp
        v_new = u_c - jnp.einsum('bhlk,bhkv->bhlv', w_c, state)
        o_c = jnp.einsum('bhlk,bhkv->bhlv', qd_c, state) + jnp.einsum('bhij,bhjv->bhiv', qk_c, v_new)
        state = state * ed_c[..., None, None] + jnp.einsum('bhlk,bhlv->bhkv', kd_c, v_new)
        return state, o_c

    xs = tuple(jnp.swapaxes(t, 0, 1) for t in (u, w, qk, q_dec, k_dec, end_dec))
    s_last, o = lax.scan(step, s0.astype(q.dtype), xs)
    o = jnp.moveaxis(jnp.moveaxis(o, 0, 1), 3, 2)
    return o.reshape(b, s, nh, dv), s_last


def mixer_even(h, w_in, w_out, q_norm, k_norm, conv_w, conv_b, a_log, dt_bias, d_skip, out_norm, ctx, n_rows):
    b, s, _ = h.shape
    q, k, v, z, xbc, dt = split_sizes(h @ w_in, EVEN_SPLITS)
    q = rms_norm(q.reshape(b, s, GQA_HEADS, HEAD_DIM), q_norm)
    k = rms_norm(k.reshape(b, s, GQA_KV_HEADS, HEAD_DIM), k_norm)
    v = v.reshape(b, s, GQA_KV_HEADS, HEAD_DIM)
    xbc = jax.nn.silu(short_conv(xbc, conv_w) + conv_b)
    xs, bm, cm = split_sizes(xbc, (SSD_INNER, SSD_GROUPS * SSD_STATE, SSD_GROUPS * SSD_STATE))
    xs = xs.reshape(b, s, SSD_HEADS, SSD_HEAD_DIM)
    bm = bm.reshape(b, s, SSD_GROUPS, SSD_STATE)
    cm = cm.reshape(b, s, SSD_GROUPS, SSD_STATE)
    dt = jax.nn.softplus(dt.reshape(b, s, 2, SSD_HEADS) + dt_bias)
    a = -jnp.exp(a_log)
    if ctx is None:
        k_all, v_all = k, v
        h0 = jnp.zeros((b, 2, SSD_HEADS, SSD_HEAD_DIM, SSD_STATE), h.dtype)
    else:
        k_ctx, v_ctx, h0 = ctx
        q = rope_2d(q, n_rows)
        k_all = jnp.concatenate([k_ctx, rope_2d(k, n_rows)], axis=1)
        v_all = jnp.concatenate([v_ctx, v], axis=1)
    attn = block_attention(q.reshape(b, s, GQA_KV_HEADS, GQA_HEADS // GQA_KV_HEADS, HEAD_DIM), k_all, v_all)
    y_f, h_f = ssd_scan(xs, dt[:, :, 0], a[0], bm, cm, h0[:, 0])
    y_b, h_b = ssd_scan(flip(xs), flip(dt[:, :, 1]), a[1], flip(bm), flip(cm), h0[:, 1])
    y = y_f + flip(y_b) + jnp.sum(d_skip, axis=0)[:, None] * xs
    y = rms_norm(y.reshape(b, s, SSD_INNER) * jax.nn.silu(z), out_norm)
    out = jnp.concatenate([attn.reshape(b, s, -1), y], axis=-1) @ w_out
    new_ctx = (k, v, jnp.stack([h_f, h_b], axis=1)) if ctx is None else None
    return out, new_ctx


def mixer_odd(h, w_in, w_out, conv_w, a_log, dt_bias, out_norm, q_a_norm, w_q_b, kv_a_norm, w_kv_b, ctx, n_rows):
    b, s, _ = h.shape
    qkv, z, gates, q_a, kv_a, k_pe = split_sizes(h @ w_in, ODD_SPLITS)
    if ctx is None:
        s0 = jnp.zeros((b, 2, GDN_HEADS, HEAD_DIM, HEAD_DIM), h.dtype)
    else:
        s0, ckv_ctx, kpe_ctx = ctx
    qkv = jax.nn.silu(short_conv(qkv, conv_w))
    gq, gk, gv = split_sizes(qkv, (GDN_WIDTH, GDN_WIDTH, GDN_WIDTH))
    gq = l2_normalize(gq.reshape(b, s, GDN_HEADS, HEAD_DIM))
    gk = l2_normalize(gk.reshape(b, s, GDN_HEADS, HEAD_DIM))
    gv = gv.reshape(b, s, GDN_HEADS, HEAD_DIM)
    gates = gates.reshape(b, s, 2, 2, GDN_HEADS)
    g = -jnp.exp(a_log) * jax.nn.softplus(gates[:, :, 0] + dt_bias)
    beta = jax.nn.sigmoid(gates[:, :, 1])
    o_f, s_f = gated_delta_chunked(gq, gk, gv, g[:, :, 0], beta[:, :, 0], s0[:, 0])
    o_b, s_b = gated_delta_chunked(flip(gq), flip(gk), flip(gv), flip(g[:, :, 1]), flip(beta[:, :, 1]), s0[:, 1])
    o = rms_norm(o_f + flip(o_b), out_norm) * jax.nn.silu(z.reshape(b, s, GDN_HEADS, HEAD_DIM))
    qm = (rms_norm(q_a, q_a_norm) @ w_q_b).reshape(b, s, MLA_HEADS, MLA_NOPE + MLA_ROPE)
    q_nope, q_pe = qm[..., :MLA_NOPE], qm[..., MLA_NOPE:]
    ckv = rms_norm(kv_a, kv_a_norm)
    if ctx is None:
        ckv_all, kpe_all = ckv, k_pe
    else:
        q_pe = rope_2d(q_pe, n_rows)
        ckv_all = jnp.concatenate([ckv_ctx, ckv], axis=1)
        kpe_all = jnp.concatenate([kpe_ctx, rope_2d(k_pe[:, :, None, :], n_rows)[:, :, 0]], axis=1)
    s_k = ckv_all.shape[1]
    kv = (ckv_all @ w_kv_b).reshape(b, s_k, MLA_HEADS, MLA_NOPE + MLA_V)
    k_full = jnp.concatenate([kv[..., :MLA_NOPE], jnp.broadcast_to(kpe_all[:, :, None, :], (b, s_k, MLA_HEADS, MLA_ROPE))], axis=-1)
    q_full = jnp.concatenate([q_nope, q_pe], axis=-1)[:, :, :, None, :]
    o_m = block_attention(q_full, k_full, kv[..., MLA_NOPE:])
    out = jnp.concatenate([o.reshape(b, s, -1), o_m.reshape(b, s, -1)], axis=-1) @ w_out
    new_ctx = (jnp.stack([s_f, s_b], axis=1), ckv, k_pe) if ctx is None else None
    return out, new_ctx


def moe_ffn(h, router_w, router_b, w_gate, b_gate, w_up, b_up, w_down, b_down):
    t, d = h.shape
    logits = (h @ router_w + router_b).astype(jnp.float32)
    top_logit, top_idx = lax.top_k(logits, TOP_K)
    gates = jax.nn.softmax(top_logit, axis=-1).astype(h.dtype)
    n_assign = t * TOP_K
    flat_e = top_idx.reshape(n_assign)
    order = jnp.argsort(flat_e)
    sorted_e = flat_e[order]
    sorted_tok = (order // TOP_K).astype(jnp.int32)
    sorted_g = gates.reshape(n_assign)[order]
    counts = jnp.bincount(flat_e, length=N_EXPERTS)
    padded = (counts + MOE_BLOCK - 1) // MOE_BLOCK * MOE_BLOCK
    pad_end = jnp.cumsum(padded)
    pad_start = pad_end - padded
    sort_start = jnp.cumsum(counts) - counts
    dest = pad_start[sorted_e] + jnp.arange(n_assign) - sort_start[sorted_e]
    n_blocks = -(-n_assign // MOE_BLOCK) + N_EXPERTS
    n_slots = n_blocks * MOE_BLOCK
    slot_tok = jnp.full((n_slots,), t, jnp.int32).at[dest].set(sorted_tok)
    slot_g = jnp.zeros((n_slots,), h.dtype).at[dest].set(sorted_g)
    block_e = jnp.minimum(jnp.searchsorted(pad_end, jnp.arange(n_blocks) * MOE_BLOCK, side='right'), N_EXPERTS - 1)
    h_slots = jnp.concatenate([h, jnp.zeros((1, d), h.dtype)], axis=0)[slot_tok].reshape(n_blocks, MOE_BLOCK, d)

    def expert_block(args):
        xb, e = args
        gate = jnp.minimum(xb @ w_gate[e] + b_gate[e], SWIGLU_LIMIT)
        up = jnp.clip(xb @ w_up[e] + b_up[e], -SWIGLU_LIMIT, SWIGLU_LIMIT)
        return ((up + 1.0) * gate * jax.nn.sigmoid(SWIGLU_ALPHA * gate)) @ w_down[e] + b_down[e]

    y = lax.map(expert_block, (h_slots, block_e)).reshape(n_slots, d) * slot_g[:, None]
    return jnp.zeros((t + 1, d), h.dtype).at[slot_tok].add(y)[:t]


def adaln_modulation(cond, w_mod, b_mod):
    m = (jax.nn.silu(cond) @ w_mod + b_mod)[:, None, :]
    return jnp.split(m, N_MOD, axis=-1)


def setup_inputs(seed: int = 0) -> dict:
    key = jax.random.key(seed)
    ks = iter(jax.random.split(key, 64))
    ne, no = (DEPTH + 1) // 2, DEPTH // 2
    d = D_MODEL

    def nrm(shape, scale=1.0):
        return jax.random.normal(next(ks), shape, jnp.float32) * scale

    def gain(shape):
        return 1.0 + 0.02 * nrm(shape)

    def a_log_init(shape):
        return jnp.log(jax.random.uniform(next(ks), shape, jnp.float32, 1.0, 16.0))

    def dt_bias_init(shape):
        dt = jnp.exp(jax.random.uniform(next(ks), shape, jnp.float32, math.log(1e-3), math.log(1e-1)))
        return dt + jnp.log(-jnp.expm1(-dt))

    return {
        'x_prompt': nrm((BATCH, SEQ, d)),
        'x_sample': nrm((DEC_BATCH, DEC_SEQ, d)),
        'cache_gqa_k': nrm((DEC_BATCH, ne, PAST_LEN, GQA_KV_HEADS, HEAD_DIM)),
        'cache_gqa_v': nrm((DEC_BATCH, ne, PAST_LEN, GQA_KV_HEADS, HEAD_DIM)),
        'state_ssd': nrm((DEC_BATCH, ne, 2, SSD_HEADS, SSD_HEAD_DIM, SSD_STATE), 0.1),
        'state_gdn': nrm((DEC_BATCH, no, 2, GDN_HEADS, HEAD_DIM, HEAD_DIM), 0.1),
        'cache_mla_ckv': nrm((DEC_BATCH, no, PAST_LEN, MLA_KV_LORA)),
        'cache_mla_kpe': nrm((DEC_BATCH, no, PAST_LEN, MLA_ROPE)),
        'c': nrm((DEC_BATCH, d)),
        'c_ctx': nrm((d,)),
        'w_mod': nrm((DEPTH, d, N_MOD * d), 0.5 * d ** -0.5),
        'b_mod': nrm((DEPTH, N_MOD * d), 0.02),
        'norm_mix': gain((DEPTH, d)),
        'norm_ffn': gain((DEPTH, d)),
        'ev_w_in': nrm((ne, d, EVEN_IN), d ** -0.5),
        'ev_w_out': nrm((ne, EVEN_OUT_IN, d), EVEN_OUT_IN ** -0.5),
        'gqa_q_norm': gain((ne, HEAD_DIM)),
        'gqa_k_norm': gain((ne, HEAD_DIM)),
        'ssd_conv_w': nrm((ne, CONV_WIDTH, SSD_CONV_CH), CONV_WIDTH ** -0.5),
        'ssd_conv_b': nrm((ne, SSD_CONV_CH), 0.02),
        'ssd_a_log': a_log_init((ne, 2, SSD_HEADS)),
        'ssd_dt_bias': dt_bias_init((ne, 2, SSD_HEADS)),
        'ssd_d_skip': 1.0 + 0.1 * nrm((ne, 2, SSD_HEADS)),
        'ssd_norm': gain((ne, SSD_INNER)),
        'od_w_in': nrm((no, d, ODD_IN), d ** -0.5),
        'od_w_out': nrm((no, ODD_OUT_IN, d), ODD_OUT_IN ** -0.5),
        'gdn_conv_w': nrm((no, CONV_WIDTH, 3 * GDN_WIDTH), CONV_WIDTH ** -0.5),
        'gdn_a_log': a_log_init((no, 2, GDN_HEADS)),
        'gdn_dt_bias': dt_bias_init((no, 2, GDN_HEADS)),
        'gdn_norm': gain((no, HEAD_DIM)),
        'mla_q_a_norm': gain((no, MLA_Q_LORA)),
        'mla_w_q_b': nrm((no, MLA_Q_LORA, MLA_HEADS * (MLA_NOPE + MLA_ROPE)), MLA_Q_LORA ** -0.5),
        'mla_kv_a_norm': gain((no, MLA_KV_LORA)),
        'mla_w_kv_b': nrm((no, MLA_KV_LORA, MLA_HEADS * (MLA_NOPE + MLA_V)), MLA_KV_LORA ** -0.5),
        'router_w': nrm((DEPTH, d, N_EXPERTS), d ** -0.5),
        'router_b': nrm((DEPTH, N_EXPERTS), 0.01),
        'moe_w_gate': nrm((DEPTH, N_EXPERTS, d, EXPERT_FF), d ** -0.5),
        'moe_b_gate': nrm((DEPTH, N_EXPERTS, EXPERT_FF), 0.02),
        'moe_w_up': nrm((DEPTH, N_EXPERTS, d, EXPERT_FF), d ** -0.5),
        'moe_b_up': nrm((DEPTH, N_EXPERTS, EXPERT_FF), 0.02),
        'moe_w_down': nrm((DEPTH, N_EXPERTS, EXPERT_FF, d), EXPERT_FF ** -0.5),
        'moe_b_down': nrm((DEPTH, N_EXPERTS, d), 0.02),
        'final_norm': gain((d,)),
    }


def reference(x_prompt, x_sample, cache_gqa_k, cache_gqa_v, state_ssd, state_gdn, cache_mla_ckv, cache_mla_kpe,
              c, c_ctx, w_mod, b_mod, norm_mix, norm_ffn, ev_w_in, ev_w_out, gqa_q_norm, gqa_k_norm,
              ssd_conv_w, ssd_conv_b, ssd_a_log, ssd_dt_bias, ssd_d_skip, ssd_norm,
              od_w_in, od_w_out, gdn_conv_w, gdn_a_log, gdn_dt_bias, gdn_norm,
              mla_q_a_norm, mla_w_q_b, mla_kv_a_norm, mla_w_kv_b,
              router_w, router_b, moe_w_gate, moe_b_gate, moe_w_up, moe_b_up, moe_w_down, moe_b_down,
              final_norm):
    n_rows = x_sample.shape[1] // GRID_W

    def run(x, cond, rows, cached):
        ctx_even, ctx_odd = [], []
        for l in range(DEPTH):
            i = l // 2
            sh1, sc1, g1, sh2, sc2, g2 = adaln_modulation(cond, w_mod[l], b_mod[l])
            h = rms_norm(x, norm_mix[l]) * (1.0 + sc1) + sh1
            if l % 2 == 0:
                ctx = None if cached is None else (cached[0][:, i], cached[1][:, i], cached[2][:, i])
                out, new_ctx = mixer_even(h, ev_w_in[i], ev_w_out[i], gqa_q_norm[i], gqa_k_norm[i],
                                          ssd_conv_w[i], ssd_conv_b[i], ssd_a_log[i], ssd_dt_bias[i],
                                          ssd_d_skip[i], ssd_norm[i], ctx, rows)
                ctx_even.append(new_ctx)
            else:
                ctx = None if cached is None else (cached[3][:, i], cached[4][:, i], cached[5][:, i])
                out, new_ctx = mixer_odd(h, od_w_in[i], od_w_out[i], gdn_conv_w[i], gdn_a_log[i], gdn_dt_bias[i],
                                         gdn_norm[i], mla_q_a_norm[i], mla_w_q_b[i], mla_kv_a_norm[i],
                                         mla_w_kv_b[i], ctx, rows)
                ctx_odd.append(new_ctx)
            x = x + g1 * out
            h = rms_norm(x, norm_ffn[l]) * (1.0 + sc2) + sh2
            ffn = moe_ffn(h.reshape(-1, D_MODEL), router_w[l], router_b[l], moe_w_gate[l], moe_b_gate[l],
                          moe_w_up[l], moe_b_up[l], moe_w_down[l], moe_b_down[l])
            x = x + g2 * ffn.reshape(x.shape)
        return rms_norm(x, final_norm), ctx_even, ctx_odd

    y_prompt, ctx_even, ctx_odd = run(x_prompt, c_ctx[None, :], None, None)
    y_sample, _, _ = run(x_sample, c, n_rows,
                         (cache_gqa_k, cache_gqa_v, state_ssd, state_gdn, cache_mla_ckv, cache_mla_kpe))
    new_gqa_k = jnp.stack([t[0] for t in ctx_even], axis=1)
    new_gqa_v = jnp.stack([t[1] for t in ctx_even], axis=1)
    new_ssd = jnp.stack([t[2] for t in ctx_even], axis=1)
    new_gdn = jnp.stack([t[0] for t in ctx_odd], axis=1)
    new_mla_ckv = jnp.stack([t[1] for t in ctx_odd], axis=1)
    new_mla_kpe = jnp.stack([t[2] for t in ctx_odd], axis=1)
    return (y_prompt, y_sample, new_gqa_k, new_gqa_v, new_ssd, new_gdn, new_mla_ckv, new_mla_kpe)
```

```python
import functools
import math

import jax
import jax.numpy as jnp
import numpy as np
from jax import lax
from jax.experimental import pallas as pl
from jax.experimental.pallas import tpu as pltpu

F32 = jnp.float32
BF16 = jnp.bfloat16

D_MODEL = 4096
BATCH = 16
SEQ = 256
DEPTH = 4
DEC_BATCH = 8
DEC_SEQ = 1024
PAST_LEN = 256
GRID_W = 64
HEAD_DIM = 128
ROPE_BASE = 10000.0
NORM_EPS = 1e-6
CONV_WIDTH = 5
GROUP_WIDTH = D_MODEL // 2
N_MOD = 6
GQA_HEADS = GROUP_WIDTH // HEAD_DIM
GQA_KV_HEADS = GQA_HEADS // 4
GQA_GROUP = GQA_HEADS // GQA_KV_HEADS
SSD_INNER = GROUP_WIDTH
SSD_HEAD_DIM = 64
SSD_HEADS = SSD_INNER // SSD_HEAD_DIM
SSD_GROUPS = 4
SSD_STATE = 128
SSD_CHUNK = 128
SSD_CONV_CH = SSD_INNER + 2 * SSD_GROUPS * SSD_STATE
GDN_HEADS = GROUP_WIDTH // HEAD_DIM
GDN_WIDTH = GDN_HEADS * HEAD_DIM
GDN_CHUNK = 64
MLA_HEADS = GROUP_WIDTH // HEAD_DIM
MLA_Q_LORA = D_MODEL // 4
MLA_KV_LORA = 512
MLA_NOPE = 128
MLA_ROPE = 64
MLA_V = 128
N_EXPERTS = 32
TOP_K = 4
EXPERT_FF = 1024
SWIGLU_LIMIT = 7.0
SWIGLU_ALPHA = 1.702
EVEN_SPLITS = (GQA_HEADS * HEAD_DIM, GQA_KV_HEADS * HEAD_DIM, GQA_KV_HEADS * HEAD_DIM, SSD_INNER, SSD_CONV_CH,
               2 * SSD_HEADS)
ODD_SPLITS = (3 * GDN_WIDTH, GDN_WIDTH, 4 * GDN_HEADS, MLA_Q_LORA, MLA_KV_LORA, MLA_ROPE)
EVEN_MAIN = sum(EVEN_SPLITS[:-1])

N_PROMPT = BATCH * SEQ
N_SAMPLE = DEC_BATCH * DEC_SEQ
N_TOK = N_PROMPT + N_SAMPLE
N_COND = 1 + DEC_BATCH
N_COND_PAD = 16

V7X_VMEM_LIMIT_BYTES = 56 * 1024 * 1024

ROW_TILE = 256
MOE_TM = 256
MOE_TF = 512
MOE_TN = 2048
MOE_NBLK = N_TOK * TOP_K // MOE_TM + N_EXPERTS
MOE_SLOTS = MOE_NBLK * MOE_TM


def _cond_row_of_tile(tile, rows_per_tile):
    row0 = tile * rows_per_tile
    return jnp.where(row0 < N_PROMPT, 0, 1 + (row0 - N_PROMPT) // DEC_SEQ)


def _params(semantics):
    return pltpu.CompilerParams(dimension_semantics=semantics, vmem_limit_bytes=V7X_VMEM_LIMIT_BYTES)


def _mm_kernel(*refs, has_bias, has_res):
    a_ref, w_ref = refs[0], refs[1]
    pos = 2
    bias_ref = res_ref = gate_ref = None
    if has_bias:
        bias_ref = refs[pos]
        pos += 1
    if has_res:
        res_ref, gate_ref = refs[pos], refs[pos + 1]
        pos += 2
    o_ref, wbf_ref = refs[pos], refs[pos + 1]

    @pl.when(pl.program_id(1) == 0)
    def _():
        wbf_ref[...] = w_ref[...].astype(BF16)

    acc = jnp.dot(a_ref[...], wbf_ref[...], preferred_element_type=F32)
    if has_bias:
        acc = acc + bias_ref[...]
    if has_res:
        acc = res_ref[...] + gate_ref[...] * acc
    o_ref[...] = acc.astype(o_ref.dtype)


def matmul(a, w, prefix=(), *, n_cols=None, bias=None, residual=None, gate=None, out_dtype=F32, tm=512, tn=512,
           name="mm"):
    m, k = a.shape
    n = w.shape[-1] if n_cols is None else n_cols
    assert w.shape[-2] == k and len(w.shape) == len(prefix) + 2
    tm = min(tm, m)
    tn = min(tn, n)
    assert m % tm == 0 and n % tn == 0
    grid = (n // tn, m // tm)
    npre = len(prefix)
    in_specs = [
        pl.BlockSpec((tm, k), lambda j, i: (i, 0)),
        pl.BlockSpec((None,) * npre + (k, tn), lambda j, i: prefix + (0, j)),
    ]
    args = [a, w]
    if bias is not None:
        b2 = bias.reshape(bias.shape[:-1] + (1, bias.shape[-1]))
        in_specs.append(pl.BlockSpec((None,) * npre + (1, tn), lambda j, i: prefix + (0, j)))
        args.append(b2)
    if residual is not None:
        in_specs.append(pl.BlockSpec((tm, tn), lambda j, i: (i, j)))
        in_specs.append(pl.BlockSpec((None, 1, tn), lambda j, i: (_cond_row_of_tile(i, tm), 0, j)))
        args += [residual, gate]
    return pl.pallas_call(
        functools.partial(_mm_kernel, has_bias=bias is not None, has_res=residual is not None),
        out_shape=jax.ShapeDtypeStruct((m, n), out_dtype),
        grid=grid,
        in_specs=in_specs,
        out_specs=pl.BlockSpec((tm, tn), lambda j, i: (i, j)),
        scratch_shapes=[pltpu.VMEM((k, tn), BF16)],
        compiler_params=_params(("arbitrary", "arbitrary")),
        name=name,
    )(*args)


def _mod_kernel(a_ref, w_ref, b_ref, o_ref):
    acc = jnp.dot(a_ref[...], w_ref[...].astype(BF16), preferred_element_type=F32)
    o_ref[...] = acc + b_ref[...]


def modulation(cond_act, w_mod, b_mod, tn=512):
    depth, k, n = w_mod.shape
    return pl.pallas_call(
        _mod_kernel,
        out_shape=jax.ShapeDtypeStruct((depth, N_COND_PAD, n), F32),
        grid=(depth, n // tn),
        in_specs=[
            pl.BlockSpec((N_COND_PAD, k), lambda l, j: (0, 0)),
            pl.BlockSpec((None, k, tn), lambda l, j: (l, 0, j)),
            pl.BlockSpec((None, 1, tn), lambda l, j: (l, 0, j)),
        ],
        out_specs=pl.BlockSpec((None, N_COND_PAD, tn), lambda l, j: (l, 0, j)),
        compiler_params=_params(("arbitrary", "arbitrary")),
        name="adaln_modulation",
    )(cond_act, w_mod, b_mod.reshape(depth, 1, n))


def _norm_kernel(*refs, modulate, router):
    x_ref, nw_ref = refs[0], refs[1]
    pos = 2
    if modulate:
        sc_ref, sh_ref = refs[pos], refs[pos + 1]
        pos += 2
    if router:
        rw_ref, rb_ref = refs[pos], refs[pos + 1]
        pos += 2
    h_ref = refs[pos]
    x = x_ref[...]
    xn = x * lax.rsqrt(jnp.mean(x * x, axis=-1, keepdims=True) + NORM_EPS)
    h = xn * nw_ref[...]
    if modulate:
        h = h * (1.0 + sc_ref[...]) + sh_ref[...]
    h_ref[...] = h.astype(h_ref.dtype)
    if router:
        logit_ref = refs[pos + 1]
        logit_ref[...] = jnp.dot(h.astype(BF16), rw_ref[...].astype(BF16), preferred_element_type=F32) + rb_ref[...]


def norm_modulate(x, norm_w, layer, scale=None, shift=None, router_w=None, router_b=None, out_dtype=BF16):
    n, d = x.shape
    tm = ROW_TILE
    modulate = scale is not None
    router = router_w is not None
    in_specs = [pl.BlockSpec((tm, d), lambda i: (i, 0)),
                pl.BlockSpec((None, 1, d), lambda i: (layer, 0, 0))]
    args = [x, norm_w.reshape(norm_w.shape[0], 1, d)]
    if modulate:
        spec = pl.BlockSpec((None, 1, d), lambda i: (_cond_row_of_tile(i, tm), 0, 0))
        in_specs += [spec, spec]
        args += [scale, shift]
    out_shape = [jax.ShapeDtypeStruct((n, d), out_dtype)]
    out_specs = [pl.BlockSpec((tm, d), lambda i: (i, 0))]
    if router:
        in_specs += [pl.BlockSpec((None, d, N_EXPERTS), lambda i: (layer, 0, 0)),
                     pl.BlockSpec((None, 1, N_EXPERTS), lambda i: (layer, 0, 0))]
        args += [router_w, router_b.reshape(router_b.shape[0], 1, N_EXPERTS)]
        out_shape.append(jax.ShapeDtypeStruct((n, N_EXPERTS), F32))
        out_specs.append(pl.BlockSpec((tm, N_EXPERTS), lambda i: (i, 0)))
    res = pl.pallas_call(
        functools.partial(_norm_kernel, modulate=modulate, router=router),
        out_shape=out_shape,
        grid=(n // tm,),
        in_specs=in_specs,
        out_specs=out_specs,
        compiler_params=_params(("arbitrary",)),
        name="norm_modulate",
    )(*args)
    return res if router else res[0]


def _softmax_pv(s, v):
    m = jnp.max(s, axis=-1, keepdims=True)
    p = jnp.exp(s - m)
    p = p / jnp.sum(p, axis=-1, keepdims=True)
    return jnp.dot(p.astype(BF16), v, preferred_element_type=F32)


_NT = (((1,), (1,)), ((), ()))


def _gqa_kernel(q_ref, k_ref, v_ref, o_ref, *, scale):
    k = k_ref[...]
    v = v_ref[...]
    for h in range(GQA_GROUP):
        cols = slice(h * HEAD_DIM, (h + 1) * HEAD_DIM)
        s = lax.dot_general(q_ref[:, cols], k, _NT, preferred_element_type=F32) * scale
        o_ref[:, cols] = _softmax_pv(s, v).astype(o_ref.dtype)


def gqa_attention(q, k, v, tq=256):
    b, s, _ = q.shape
    sk = k.shape[1]
    gw = GQA_GROUP * HEAD_DIM
    return pl.pallas_call(
        functools.partial(_gqa_kernel, scale=HEAD_DIM ** -0.5),
        out_shape=jax.ShapeDtypeStruct(q.shape, BF16),
        grid=(b, GQA_KV_HEADS, s // tq),
        in_specs=[
            pl.BlockSpec((None, tq, gw), lambda bi, j, qi: (bi, qi, j)),
            pl.BlockSpec((None, sk, HEAD_DIM), lambda bi, j, qi: (bi, 0, j)),
            pl.BlockSpec((None, sk, HEAD_DIM), lambda bi, j, qi: (bi, 0, j)),
        ],
        out_specs=pl.BlockSpec((None, tq, gw), lambda bi, j, qi: (bi, qi, j)),
        compiler_params=_params(("arbitrary", "arbitrary", "arbitrary")),
        name="gqa_attention",
    )(q, k, v)


MLA_HEADS_PER_STEP = 2


def _mla_kernel(qn_ref, qp_ref, kv_ref, kpe_ref, o_ref, *, scale):
    kpe = kpe_ref[...]
    for h in range(MLA_HEADS_PER_STEP):
        kcols = slice(h * (MLA_NOPE + MLA_V), h * (MLA_NOPE + MLA_V) + MLA_NOPE)
        vcols = slice(h * (MLA_NOPE + MLA_V) + MLA_NOPE, (h + 1) * (MLA_NOPE + MLA_V))
        s = lax.dot_general(qn_ref[:, h * MLA_NOPE:(h + 1) * MLA_NOPE], kv_ref[:, kcols], _NT,
                            preferred_element_type=F32)
        s = s + lax.dot_general(qp_ref[:, h * MLA_ROPE:(h + 1) * MLA_ROPE], kpe, _NT,
                                preferred_element_type=F32)
        o_ref[:, h * MLA_V:(h + 1) * MLA_V] = _softmax_pv(s * scale, kv_ref[:, vcols]).astype(o_ref.dtype)


def mla_attention(q_nope, q_pe, kv, k_pe, tq=256):
    b, s, _ = q_nope.shape
    sk = kv.shape[1]
    hp = MLA_HEADS_PER_STEP
    return pl.pallas_call(
        functools.partial(_mla_kernel, scale=(MLA_NOPE + MLA_ROPE) ** -0.5),
        out_shape=jax.ShapeDtypeStruct((b, s, MLA_HEADS * MLA_V), BF16),
        grid=(b, MLA_HEADS // hp, s // tq),
        in_specs=[
            pl.BlockSpec((None, tq, hp * MLA_NOPE), lambda bi, g, qi: (bi, qi, g)),
            pl.BlockSpec((None, tq, hp * MLA_ROPE), lambda bi, g, qi: (bi, qi, g)),
            pl.BlockSpec((None, sk, hp * (MLA_NOPE + MLA_V)), lambda bi, g, qi: (bi, 0, g)),
            pl.BlockSpec((None, sk, MLA_ROPE), lambda bi, g, qi: (bi, 0, 0)),
        ],
        out_specs=pl.BlockSpec((None, tq, hp * MLA_V), lambda bi, g, qi: (bi, qi, g)),
        compiler_params=_params(("arbitrary", "arbitrary", "arbitrary")),
        name="mla_attention",
    )(q_nope, q_pe, kv, k_pe)


ITEM_VALID = 1
ITEM_NEW_WEIGHTS = 2


def _moe_up_kernel(ib_ref, ic_ref, iwe_ref, iwc_ref, iflag_ref, x_ref, wg_ref, wu_ref, bg_ref, bu_ref, o_ref,
                   wg_bf, wu_bf):
    flag = iflag_ref[pl.program_id(0)]

    @pl.when((flag & ITEM_NEW_WEIGHTS) != 0)
    def _():
        wg_bf[...] = wg_ref[...].astype(BF16)
        wu_bf[...] = wu_ref[...].astype(BF16)

    @pl.when((flag & ITEM_VALID) != 0)
    def _():
        x = x_ref[...]
        gate = jnp.minimum(jnp.dot(x, wg_bf[...], preferred_element_type=F32) + bg_ref[...], SWIGLU_LIMIT)
        up = jnp.clip(jnp.dot(x, wu_bf[...], preferred_element_type=F32) + bu_ref[...], -SWIGLU_LIMIT, SWIGLU_LIMIT)
        act = (up + 1.0) * gate * (1.0 / (1.0 + jnp.exp(-SWIGLU_ALPHA * gate)))
        o_ref[...] = act.astype(o_ref.dtype)

    @pl.when((flag & ITEM_VALID) == 0)
    def _():
        o_ref[...] = jnp.zeros_like(o_ref)


def _moe_down_kernel(ib_ref, ic_ref, iwe_ref, iwc_ref, iflag_ref, a_ref, wd_ref, bd_ref, g_ref, o_ref, wd_bf):
    flag = iflag_ref[pl.program_id(0)]

    @pl.when((flag & ITEM_NEW_WEIGHTS) != 0)
    def _():
        wd_bf[...] = wd_ref[...].astype(BF16)

    @pl.when((flag & ITEM_VALID) != 0)
    def _():
        y = jnp.dot(a_ref[...], wd_bf[...], preferred_element_type=F32) + bd_ref[...]
        o_ref[...] = y * g_ref[...]

    @pl.when((flag & ITEM_VALID) == 0)
    def _():
        o_ref[...] = jnp.zeros_like(o_ref)


def _moe_items(blk_start, nblk, n_used, n_chunks):
    n_items = MOE_NBLK * n_chunks
    i = jnp.arange(n_items, dtype=jnp.int32)
    item_end = (blk_start + nblk) * n_chunks
    e = jnp.minimum(jnp.searchsorted(item_end, i, side='right'), N_EXPERTS - 1).astype(jnp.int32)
    r = i - blk_start[e] * n_chunks
    nb = jnp.maximum(nblk[e], 1)
    valid = i < n_used * n_chunks
    n_tail = jnp.maximum(MOE_NBLK - n_used, 1)
    rt = i - n_used * n_chunks
    chunk = jnp.where(valid, r // nb, rt // n_tail).astype(jnp.int32)
    blk = jnp.where(valid, blk_start[e] + r % nb, n_used + rt % n_tail).astype(jnp.int32)
    last = jnp.maximum(n_used * n_chunks - 1, 0)
    we = jnp.where(valid, e, e[last]).astype(jnp.int32)
    wc = jnp.where(valid, chunk, chunk[last]).astype(jnp.int32)
    new_w = jnp.concatenate([jnp.ones((1,), bool), (we[1:] != we[:-1]) | (wc[1:] != wc[:-1])])
    flag = valid.astype(jnp.int32) * ITEM_VALID + new_w.astype(jnp.int32) * ITEM_NEW_WEIGHTS
    return blk, chunk, we, wc, flag


def moe_ffn(h, logits, layer, w_gate, b_gate, w_up, b_up, w_down, b_down):
    t, d = h.shape
    n_assign = t * TOP_K
    top_logit, top_idx = lax.top_k(logits, TOP_K)
    gates = jax.nn.softmax(top_logit, axis=-1)
    flat_e = top_idx.reshape(n_assign).astype(jnp.int32)
    order = jnp.argsort(flat_e).astype(jnp.int32)
    sorted_e = flat_e[order]
    counts = jnp.bincount(flat_e, length=N_EXPERTS).astype(jnp.int32)
    nblk = (counts + MOE_TM - 1) // MOE_TM
    blk_end = jnp.cumsum(nblk).astype(jnp.int32)
    blk_start = blk_end - nblk
    n_used = blk_end[-1]
    sort_start = jnp.cumsum(counts).astype(jnp.int32) - counts
    dest = blk_start[sorted_e] * MOE_TM + jnp.arange(n_assign, dtype=jnp.int32) - sort_start[sorted_e]
    slot_tok = jnp.full((MOE_SLOTS,), t, jnp.int32).at[dest].set(order // TOP_K)
    slot_g = jnp.zeros((MOE_SLOTS,), F32).at[dest].set(gates.reshape(n_assign)[order])
    slot_of_assign = jnp.zeros((n_assign,), jnp.int32).at[order].set(dest).reshape(t, TOP_K)
    x_slots = jnp.concatenate([h, jnp.zeros((1, d), h.dtype)], axis=0)[slot_tok]

    depth = w_gate.shape[0]
    n_fc = EXPERT_FF // MOE_TF
    items = _moe_items(blk_start, nblk, n_used, n_fc)
    act = pl.pallas_call(
        _moe_up_kernel,
        out_shape=jax.ShapeDtypeStruct((MOE_SLOTS, EXPERT_FF), BF16),
        grid_spec=pltpu.PrefetchScalarGridSpec(
            num_scalar_prefetch=5,
            grid=(MOE_NBLK * n_fc,),
            in_specs=[
                pl.BlockSpec((MOE_TM, d), lambda i, ib, ic, iwe, iwc, ifl: (ib[i], 0)),
                pl.BlockSpec((None, None, d, MOE_TF), lambda i, ib, ic, iwe, iwc, ifl: (layer, iwe[i], 0, iwc[i])),
                pl.BlockSpec((None, None, d, MOE_TF), lambda i, ib, ic, iwe, iwc, ifl: (layer, iwe[i], 0, iwc[i])),
                pl.BlockSpec((None, None, 1, MOE_TF), lambda i, ib, ic, iwe, iwc, ifl: (layer, iwe[i], 0, iwc[i])),
                pl.BlockSpec((None, None, 1, MOE_TF), lambda i, ib, ic, iwe, iwc, ifl: (layer, iwe[i], 0, iwc[i])),
            ],
            out_specs=pl.BlockSpec((MOE_TM, MOE_TF), lambda i, ib, ic, iwe, iwc, ifl: (ib[i], ic[i])),
            scratch_shapes=[pltpu.VMEM((d, MOE_TF), BF16), pltpu.VMEM((d, MOE_TF), BF16)],
        ),
        compiler_params=_params(("arbitrary",)),
        name="moe_gate_up",
    )(*items, x_slots, w_gate, w_up,
      b_gate.reshape(depth, N_EXPERTS, 1, EXPERT_FF), b_up.reshape(depth, N_EXPERTS, 1, EXPERT_FF))

    n_nc = d // MOE_TN
    items = _moe_items(blk_start, nblk, n_used, n_nc)
    y = pl.pallas_call(
        _moe_down_kernel,
        out_shape=jax.ShapeDtypeStruct((MOE_SLOTS, d), F32),
        grid_spec=pltpu.PrefetchScalarGridSpec(
            num_scalar_prefetch=5,
            grid=(MOE_NBLK * n_nc,),
            in_specs=[
                pl.BlockSpec((MOE_TM, EXPERT_FF), lambda i, ib, ic, iwe, iwc, ifl: (ib[i], 0)),
                pl.BlockSpec((None, None, EXPERT_FF, MOE_TN),
                             lambda i, ib, ic, iwe, iwc, ifl: (layer, iwe[i], 0, iwc[i])),
                pl.BlockSpec((None, None, 1, MOE_TN), lambda i, ib, ic, iwe, iwc, ifl: (layer, iwe[i], 0, iwc[i])),
                pl.BlockSpec((MOE_TM, 1), lambda i, ib, ic, iwe, iwc, ifl: (ib[i], 0)),
            ],
            out_specs=pl.BlockSpec((MOE_TM, MOE_TN), lambda i, ib, ic, iwe, iwc, ifl: (ib[i], ic[i])),
            scratch_shapes=[pltpu.VMEM((EXPERT_FF, MOE_TN), BF16)],
        ),
        compiler_params=_params(("arbitrary",)),
        name="moe_down",
    )(*items, act, w_down, b_down.reshape(depth, N_EXPERTS, 1, d), slot_g.reshape(MOE_SLOTS, 1))
    return jnp.sum(y[slot_of_assign], axis=1)


def _split(x, sizes):
    return jnp.split(x, np.cumsum(sizes)[:-1].tolist(), axis=-1)


def _flip(x):
    return jnp.flip(x, axis=1)


def _rms(x, w):
    return (x * lax.rsqrt(jnp.mean(x * x, axis=-1, keepdims=True) + NORM_EPS)) * w


def _l2n(x):
    return x * lax.rsqrt(jnp.sum(x * x, axis=-1, keepdims=True) + NORM_EPS)


def _rope_1d(x, pos):
    half = x.shape[-1] // 2
    inv_freq = ROPE_BASE ** (-jnp.arange(half, dtype=F32) / half)
    ang = pos.astype(F32)[:, None] * inv_freq
    cos = jnp.cos(ang)[:, None, :]
    sin = jnp.sin(ang)[:, None, :]
    x1, x2 = x[..., :half], x[..., half:]
    return jnp.concatenate([x1 * cos - x2 * sin, x2 * cos + x1 * sin], axis=-1)


def _rope_2d(x, n_rows):
    t = jnp.arange(n_rows * GRID_W)
    d_axis = x.shape[-1] // 2
    return jnp.concatenate([_rope_1d(x[..., :d_axis], t // GRID_W), _rope_1d(x[..., d_axis:], t % GRID_W)], axis=-1)


def _short_conv(x, w):
    pad = (w.shape[0] - 1) // 2
    s = x.shape[1]
    xp = jnp.pad(x, ((0, 0), (pad, pad), (0, 0)))
    out = xp[:, 0:s] * w[0]
    for j in range(1, w.shape[0]):
        out = out + xp[:, j:j + s] * w[j]
    return out


def _ssd_scan(x, dt, a, bm, cm, h0):
    b, s, nh, p = x.shape
    ng, n = bm.shape[-2:]
    L = SSD_CHUNK
    nc = s // L
    rep = nh // ng
    bc = jnp.repeat(bm, rep, axis=2).reshape(b, nc, L, nh, n)
    cc = jnp.repeat(cm, rep, axis=2).reshape(b, nc, L, nh, n)
    xdt = (x * dt[..., None]).reshape(b, nc, L, nh, p)
    cs = jnp.cumsum((dt * a).reshape(b, nc, L, nh), axis=2)
    causal = jnp.tril(jnp.ones((L, L), bool))[:, :, None]
    decay = jnp.exp(jnp.where(causal, cs[:, :, :, None, :] - cs[:, :, None, :, :], -jnp.inf))
    y_intra = jnp.einsum('bcijh,bcjhp->bcihp', jnp.einsum('bcihn,bcjhn->bcijh', cc, bc) * decay, xdt)
    chunk_states = jnp.einsum('bclhn,bclhp->bchpn', bc * jnp.exp(cs[:, :, -1:, :] - cs)[..., None], xdt)
    chunk_decay = jnp.exp(cs[:, :, -1, :])

    def step(state, inp):
        st, dec = inp
        return state * dec[:, :, None, None] + st, state

    h_last, h_prev = lax.scan(step, h0, (jnp.swapaxes(chunk_states, 0, 1), jnp.swapaxes(chunk_decay, 0, 1)))
    h_prev = jnp.swapaxes(h_prev, 0, 1)
    y_inter = jnp.einsum('bclhn,bchpn->bclhp', cc * jnp.exp(cs)[..., None], h_prev)
    return (y_intra + y_inter).reshape(b, s, nh, p), h_last


def _gated_delta(q, k, v, g, beta, s0):
    b, s, nh, dk = q.shape
    dv = v.shape[-1]
    L = GDN_CHUNK
    nc = s // L

    def chunks(t):
        return jnp.moveaxis(t.reshape(b, nc, L, *t.shape[2:]), 2, 3)

    qc, kc, vc = chunks(q * dk ** -0.5), chunks(k), chunks(v)
    gcum = jnp.cumsum(chunks(g), axis=-1)
    bc = chunks(beta)
    incl = jnp.tril(jnp.ones((L, L), bool))
    strict = jnp.tril(jnp.ones((L, L), bool), -1)
    gamma = jnp.exp(jnp.where(incl, gcum[..., :, None] - gcum[..., None, :], -jnp.inf))
    a_mat = jnp.where(strict, jnp.einsum('bnhik,bnhjk->bnhij', kc, kc) * gamma * bc[..., :, None], 0.0)
    m = a_mat + jnp.eye(L, dtype=a_mat.dtype)

    def solve(rhs):
        return lax.linalg.triangular_solve(m, rhs, left_side=True, lower=True, unit_diagonal=True)

    u = solve(vc * bc[..., None])
    w = solve(kc * (bc * jnp.exp(gcum))[..., None])
    qk = jnp.einsum('bnhik,bnhjk->bnhij', qc, kc) * gamma
    q_dec = qc * jnp.exp(gcum)[..., None]
    k_dec = kc * jnp.exp(gcum[..., -1:] - gcum)[..., None]
    end_dec = jnp.exp(gcum[..., -1])

    def step(state, inp):
        u_c, w_c, qk_c, qd_c, kd_c, ed_c = inp
        v_new = u_c - jnp.einsum('bhlk,bhkv->bhlv', w_c, state)
        o_c = jnp.einsum('bhlk,bhkv->bhlv', qd_c, state) + jnp.einsum('bhij,bhjv->bhiv', qk_c, v_new)
        state = state * ed_c[..., None, None] + jnp.einsum('bhlk,bhlv->bhkv', kd_c, v_new)
        return state, o_c

    xs = tuple(jnp.swapaxes(t, 0, 1) for t in (u, w, qk, q_dec, k_dec, end_dec))
    s_last, o = lax.scan(step, s0, xs)
    o = jnp.moveaxis(jnp.moveaxis(o, 0, 1), 3, 2)
    return o.reshape(b, s, nh, dv), s_last


def _even_mixer_pass(proj, dt, q_norm, k_norm, conv_w, conv_b, a_log, dt_bias, d_skip, out_norm, ctx, n_rows):
    b, s, _ = proj.shape
    q, k, v, z, xbc = _split(proj, EVEN_SPLITS[:-1])
    q = _rms(q.reshape(b, s, GQA_HEADS, HEAD_DIM), q_norm)
    k = _rms(k.reshape(b, s, GQA_KV_HEADS, HEAD_DIM), k_norm)
    v = v.reshape(b, s, GQA_KV_HEADS, HEAD_DIM)
    xbc = jax.nn.silu(_short_conv(xbc, conv_w) + conv_b)
    xs, bm, cm = _split(xbc, (SSD_INNER, SSD_GROUPS * SSD_STATE, SSD_GROUPS * SSD_STATE))
    xs = xs.reshape(b, s, SSD_HEADS, SSD_HEAD_DIM)
    bm = bm.reshape(b, s, SSD_GROUPS, SSD_STATE)
    cm = cm.reshape(b, s, SSD_GROUPS, SSD_STATE)
    dt = jax.nn.softplus(dt.reshape(b, s, 2, SSD_HEADS) + dt_bias)
    a = -jnp.exp(a_log)
    if ctx is None:
        k_all, v_all = k, v
        h0 = jnp.zeros((b, 2, SSD_HEADS, SSD_HEAD_DIM, SSD_STATE), F32)
    else:
        k_ctx, v_ctx, h0 = ctx
        q = _rope_2d(q, n_rows)
        k_all = jnp.concatenate([k_ctx, _rope_2d(k, n_rows)], axis=1)
        v_all = jnp.concatenate([v_ctx, v], axis=1)
    sk = k_all.shape[1]
    attn = gqa_attention(q.reshape(b, s, -1).astype(BF16), k_all.reshape(b, sk, -1).astype(BF16),
                         v_all.reshape(b, sk, -1).astype(BF16))
    y_f, h_f = _ssd_scan(xs, dt[:, :, 0], a[0], bm, cm, h0[:, 0])
    y_b, h_b = _ssd_scan(_flip(xs), _flip(dt[:, :, 1]), a[1], _flip(bm), _flip(cm), h0[:, 1])
    y = y_f + _flip(y_b) + jnp.sum(d_skip, axis=0)[:, None] * xs
    y = _rms(y.reshape(b, s, SSD_INNER) * jax.nn.silu(z), out_norm)
    feat = jnp.concatenate([attn, y.astype(BF16)], axis=-1)
    new_ctx = (k, v, jnp.stack([h_f, h_b], axis=1)) if ctx is None else None
    return feat, new_ctx


def _odd_pre_pass(proj, conv_w, a_log, dt_bias, out_norm, q_a_norm, kv_a_norm, s0):
    b, s, _ = proj.shape
    qkv, z, gates, q_a, kv_a, k_pe = _split(proj, ODD_SPLITS)
    qkv = jax.nn.silu(_short_conv(qkv, conv_w))
    gq, gk, gv = _split(qkv, (GDN_WIDTH, GDN_WIDTH, GDN_WIDTH))
    gq = _l2n(gq.reshape(b, s, GDN_HEADS, HEAD_DIM))
    gk = _l2n(gk.reshape(b, s, GDN_HEADS, HEAD_DIM))
    gv = gv.reshape(b, s, GDN_HEADS, HEAD_DIM)
    gates = gates.reshape(b, s, 2, 2, GDN_HEADS)
    g = -jnp.exp(a_log) * jax.nn.softplus(gates[:, :, 0] + dt_bias)
    beta = jax.nn.sigmoid(gates[:, :, 1])
    o_f, s_f = _gated_delta(gq, gk, gv, g[:, :, 0], beta[:, :, 0], s0[:, 0])
    o_b, s_b = _gated_delta(_flip(gq), _flip(gk), _flip(gv), _flip(g[:, :, 1]), _flip(beta[:, :, 1]), s0[:, 1])
    o = _rms(o_f + _flip(o_b), out_norm) * jax.nn.silu(z.reshape(b, s, GDN_HEADS, HEAD_DIM))
    return (o.reshape(b, s, -1).astype(BF16), _rms(q_a, q_a_norm).astype(BF16), _rms(kv_a, kv_a_norm), k_pe,
            jnp.stack([s_f, s_b], axis=1))


def kernel(x_prompt, x_sample, cache_gqa_k, cache_gqa_v, state_ssd, state_gdn, cache_mla_ckv, cache_mla_kpe, c, c_ctx, w_mod, b_mod, norm_mix, norm_ffn, ev_w_in, ev_w_out, gqa_q_norm, gqa_k_norm, ssd_conv_w, ssd_conv_b, ssd_a_log, ssd_dt_bias, ssd_d_skip, ssd_norm, od_w_in, od_w_out, gdn_conv_w, gdn_a_log, gdn_dt_bias, gdn_norm, mla_q_a_norm, mla_w_q_b, mla_kv_a_norm, mla_w_kv_b, router_w, router_b, moe_w_gate, moe_b_gate, moe_w_up, moe_b_up, moe_w_down, moe_b_down, final_norm):
    n_rows = DEC_SEQ // GRID_W
    x = jnp.concatenate([x_prompt.reshape(N_PROMPT, D_MODEL), x_sample.reshape(N_SAMPLE, D_MODEL)], axis=0)

    cond = jnp.concatenate([c_ctx[None, :], c, jnp.zeros((N_COND_PAD - N_COND, D_MODEL), F32)], axis=0)
    mods = modulation(jax.nn.silu(cond).astype(BF16), w_mod, b_mod)
    mods = mods.reshape(DEPTH, N_COND_PAD, N_MOD, 1, D_MODEL)

    def split_passes(t):
        return (t[:N_PROMPT].reshape(BATCH, SEQ, -1), t[N_PROMPT:].reshape(DEC_BATCH, DEC_SEQ, -1))

    def merge_passes(tp, ts):
        return jnp.concatenate([tp.reshape(N_PROMPT, -1), ts.reshape(N_SAMPLE, -1)], axis=0)

    ctx_even, ctx_odd = [], []
    for l in range(DEPTH):
        i = l // 2
        sh1, sc1, g1, sh2, sc2, g2 = (mods[l, :, j] for j in range(N_MOD))
        h = norm_modulate(x, norm_mix, l, sc1, sh1)
        if l % 2 == 0:
            proj = matmul(h, ev_w_in, (i,), n_cols=EVEN_MAIN, name="even_in_proj")
            dt_proj = matmul(h, ev_w_in[i, :, EVEN_MAIN:], name="even_dt_proj")
            pp, ps = split_passes(proj)
            dp, ds = split_passes(dt_proj)
            args = (gqa_q_norm[i], gqa_k_norm[i], ssd_conv_w[i], ssd_conv_b[i], ssd_a_log[i], ssd_dt_bias[i],
                    ssd_d_skip[i], ssd_norm[i])
            feat_p, new_ctx = _even_mixer_pass(pp, dp, *args, None, None)
            feat_s, _ = _even_mixer_pass(ps, ds, *args, (cache_gqa_k[:, i], cache_gqa_v[:, i], state_ssd[:, i]),
                                         n_rows)
            ctx_even.append(new_ctx)
            x = matmul(merge_passes(feat_p, feat_s), ev_w_out, (i,), residual=x, gate=g1, name="even_out_proj")
        else:
            proj = matmul(h, od_w_in, (i,), tn=896, name="odd_in_proj")
            pp, ps = split_passes(proj)
            args = (gdn_conv_w[i], gdn_a_log[i], gdn_dt_bias[i], gdn_norm[i], mla_q_a_norm[i], mla_kv_a_norm[i])
            o_p, qa_p, ckv_p, kpe_p, s_p = _odd_pre_pass(
                pp, *args, jnp.zeros((BATCH, 2, GDN_HEADS, HEAD_DIM, HEAD_DIM), F32))
            o_s, qa_s, ckv_s, kpe_s, _ = _odd_pre_pass(ps, *args, state_gdn[:, i])
            ctx_odd.append((s_p, ckv_p, kpe_p))
            qm = matmul(merge_passes(qa_p, qa_s), mla_w_q_b, (i,), name="mla_q_b")
            ckv_all_s = jnp.concatenate([cache_mla_ckv[:, i], ckv_s], axis=1)
            sk_s = PAST_LEN + DEC_SEQ
            kv = matmul(jnp.concatenate([ckv_p.reshape(N_PROMPT, -1), ckv_all_s.reshape(DEC_BATCH * sk_s, -1)],
                                        axis=0).astype(BF16),
                        mla_w_kv_b, (i,), out_dtype=BF16, name="mla_kv_b")
            kv_p = kv[:N_PROMPT].reshape(BATCH, SEQ, -1)
            kv_s = kv[N_PROMPT:].reshape(DEC_BATCH, sk_s, -1)
            qm_p, qm_s = split_passes(qm)
            qm_p = qm_p.reshape(BATCH, SEQ, MLA_HEADS, MLA_NOPE + MLA_ROPE)
            qm_s = qm_s.reshape(DEC_BATCH, DEC_SEQ, MLA_HEADS, MLA_NOPE + MLA_ROPE)
            qpe_s = _rope_2d(qm_s[..., MLA_NOPE:], n_rows)
            kpe_all_s = jnp.concatenate([cache_mla_kpe[:, i], _rope_2d(kpe_s[:, :, None, :], n_rows)[:, :, 0]],
                                        axis=1)
            om_p = mla_attention(qm_p[..., :MLA_NOPE].reshape(BATCH, SEQ, -1).astype(BF16),
                                 qm_p[..., MLA_NOPE:].reshape(BATCH, SEQ, -1).astype(BF16), kv_p,
                                 kpe_p.astype(BF16))
            om_s = mla_attention(qm_s[..., :MLA_NOPE].reshape(DEC_BATCH, DEC_SEQ, -1).astype(BF16),
                                 qpe_s.reshape(DEC_BATCH, DEC_SEQ, -1).astype(BF16), kv_s, kpe_all_s.astype(BF16))
            feat = merge_passes(jnp.concatenate([o_p, om_p], axis=-1), jnp.concatenate([o_s, om_s], axis=-1))
            x = matmul(feat, od_w_out, (i,), residual=x, gate=g1, name="odd_out_proj")
        h, logits = norm_modulate(x, norm_ffn, l, sc2, sh2, router_w, router_b)
        ffn = moe_ffn(h, logits, l, moe_w_gate, moe_b_gate, moe_w_up, moe_b_up, moe_w_down, moe_b_down)
        gate_rows = jnp.concatenate([jnp.broadcast_to(g2[0], (N_PROMPT, D_MODEL)),
                                     jnp.repeat(g2[1:N_COND, 0], DEC_SEQ, axis=0)], axis=0)
        x = x + gate_rows * ffn

    y = norm_modulate(x, final_norm[None, :], 0, out_dtype=F32)
    y_prompt = y[:N_PROMPT].reshape(BATCH, SEQ, D_MODEL)
    y_sample = y[N_PROMPT:].reshape(DEC_BATCH, DEC_SEQ, D_MODEL)
    new_gqa_k = jnp.stack([t[0] for t in ctx_even], axis=1)
    new_gqa_v = jnp.stack([t[1] for t in ctx_even], axis=1)
    new_ssd = jnp.stack([t[2] for t in ctx_even], axis=1)
    new_gdn = jnp.stack([t[0] for t in ctx_odd], axis=1)
    new_mla_ckv = jnp.stack([t[1] for t in ctx_odd], axis=1)
    new_mla_kpe = jnp.stack([t[2] for t in ctx_odd], axis=1)
    return (y_prompt, y_sample, new_gqa_k, new_gqa_v, new_ssd, new_gdn, new_mla_ckv, new_mla_kpe)
```

```python
import functools
import math

import jax
import jax.numpy as jnp
import numpy as np
from jax import lax
from jax.experimental import pallas as pl
from jax.experimental.pallas import tpu as pltpu

F32 = jnp.float32
BF16 = jnp.bfloat16

D_MODEL = 4096
BATCH = 16
SEQ = 256
DEPTH = 4
DEC_BATCH = 8
DEC_SEQ = 1024
PAST_LEN = 256
GRID_W = 64
HEAD_DIM = 128
ROPE_BASE = 10000.0
NORM_EPS = 1e-6
CONV_WIDTH = 5
GROUP_WIDTH = D_MODEL // 2
N_MOD = 6
GQA_HEADS = GROUP_WIDTH // HEAD_DIM
GQA_KV_HEADS = GQA_HEADS // 4
GQA_GROUP = GQA_HEADS // GQA_KV_HEADS
SSD_INNER = GROUP_WIDTH
SSD_HEAD_DIM = 64
SSD_HEADS = SSD_INNER // SSD_HEAD_DIM
SSD_GROUPS = 4
SSD_STATE = 128
SSD_CHUNK = 128
SSD_CONV_CH = SSD_INNER + 2 * SSD_GROUPS * SSD_STATE
GDN_HEADS = GROUP_WIDTH // HEAD_DIM
GDN_WIDTH = GDN_HEADS * HEAD_DIM
GDN_CHUNK = 64
MLA_HEADS = GROUP_WIDTH // HEAD_DIM
MLA_Q_LORA = D_MODEL // 4
MLA_KV_LORA = 512
MLA_NOPE = 128
MLA_ROPE = 64
MLA_V = 128
N_EXPERTS = 32
TOP_K = 4
EXPERT_FF = 1024
SWIGLU_LIMIT = 7.0
SWIGLU_ALPHA = 1.702
EVEN_SPLITS = (GQA_HEADS * HEAD_DIM, GQA_KV_HEADS * HEAD_DIM, GQA_KV_HEADS * HEAD_DIM, SSD_INNER, SSD_CONV_CH,
               2 * SSD_HEADS)
ODD_SPLITS = (3 * GDN_WIDTH, GDN_WIDTH, 4 * GDN_HEADS, MLA_Q_LORA, MLA_KV_LORA, MLA_ROPE)
EVEN_MAIN = sum(EVEN_SPLITS[:-1])

N_PROMPT = BATCH * SEQ
N_SAMPLE = DEC_BATCH * DEC_SEQ
N_TOK = N_PROMPT + N_SAMPLE
N_COND = 1 + DEC_BATCH
N_COND_PAD = 16

V7X_VMEM_LIMIT_BYTES = 56 * 1024 * 1024

ROW_TILE = 256
MOE_TM = 256
MOE_TF = 512
MOE_TN = 2048
MOE_NBLK = N_TOK * TOP_K // MOE_TM + N_EXPERTS
MOE_SLOTS = MOE_NBLK * MOE_TM


def _cond_row_of_tile(tile, rows_per_tile):
    row0 = tile * rows_per_tile
    return jnp.where(row0 < N_PROMPT, 0, 1 + (row0 - N_PROMPT) // DEC_SEQ)


def _params(semantics):
    return pltpu.CompilerParams(dimension_semantics=semantics, vmem_limit_bytes=V7X_VMEM_LIMIT_BYTES)


def _mm_kernel(*refs, has_bias, has_res):
    a_ref, w_ref = refs[0], refs[1]
    pos = 2
    bias_ref = res_ref = gate_ref = None
    if has_bias:
        bias_ref = refs[pos]
        pos += 1
    if has_res:
        res_ref, gate_ref = refs[pos], refs[pos + 1]
        pos += 2
    o_ref, wbf_ref = refs[pos], refs[pos + 1]

    @pl.when(pl.program_id(1) == 0)
    def _():
        wbf_ref[...] = w_ref[...].astype(BF16)

    acc = jnp.dot(a_ref[...], wbf_ref[...], preferred_element_type=F32)
    if has_bias:
        acc = acc + bias_ref[...]
    if has_res:
        acc = res_ref[...] + gate_ref[...] * acc
    o_ref[...] = acc.astype(o_ref.dtype)


def matmul(a, w, prefix=(), *, n_cols=None, bias=None, residual=None, gate=None, out_dtype=F32, tm=512, tn=512,
           name="mm"):
    m, k = a.shape
    n = w.shape[-1] if n_cols is None else n_cols
    assert w.shape[-2] == k and len(w.shape) == len(prefix) + 2
    tm = min(tm, m)
    tn = min(tn, n)
    assert m % tm == 0 and n % tn == 0
    grid = (n // tn, m // tm)
    npre = len(prefix)
    in_specs = [
        pl.BlockSpec((tm, k), lambda j, i: (i, 0)),
        pl.BlockSpec((None,) * npre + (k, tn), lambda j, i: prefix + (0, j)),
    ]
    args = [a, w]
    if bias is not None:
        b2 = bias.reshape(bias.shape[:-1] + (1, bias.shape[-1]))
        in_specs.append(pl.BlockSpec((None,) * npre + (1, tn), lambda j, i: prefix + (0, j)))
        args.append(b2)
    if residual is not None:
        in_specs.append(pl.BlockSpec((tm, tn), lambda j, i: (i, j)))
        in_specs.append(pl.BlockSpec((None, 1, tn), lambda j, i: (_cond_row_of_tile(i, tm), 0, j)))
        args += [residual, gate]
    return pl.pallas_call(
        functools.partial(_mm_kernel, has_bias=bias is not None, has_res=residual is not None),
        out_shape=jax.ShapeDtypeStruct((m, n), out_dtype),
        grid=grid,
        in_specs=in_specs,
        out_specs=pl.BlockSpec((tm, tn), lambda j, i: (i, j)),
        scratch_shapes=[pltpu.VMEM((k, tn), BF16)],
        compiler_params=_params(("arbitrary", "arbitrary")),
        name=name,
    )(*args)


def _mod_kernel(a_ref, w_ref, b_ref, o_ref):
    acc = jnp.dot(a_ref[...], w_ref[...].astype(BF16), preferred_element_type=F32)
    o_ref[...] = acc + b_ref[...]


def modulation(cond_act, w_mod, b_mod, tn=512):
    depth, k, n = w_mod.shape
    return pl.pallas_call(
        _mod_kernel,
        out_shape=jax.ShapeDtypeStruct((depth, N_COND_PAD, n), F32),
        grid=(depth, n // tn),
        in_specs=[
            pl.BlockSpec((N_COND_PAD, k), lambda l, j: (0, 0)),
            pl.BlockSpec((None, k, tn), lambda l, j: (l, 0, j)),
            pl.BlockSpec((None, 1, tn), lambda l, j: (l, 0, j)),
        ],
        out_specs=pl.BlockSpec((None, N_COND_PAD, tn), lambda l, j: (l, 0, j)),
        compiler_params=_params(("arbitrary", "arbitrary")),
        name="adaln_modulation",
    )(cond_act, w_mod, b_mod.reshape(depth, 1, n))


def _norm_kernel(*refs, modulate, router):
    x_ref, nw_ref = refs[0], refs[1]
    pos = 2
    if modulate:
        sc_ref, sh_ref = refs[pos], refs[pos + 1]
        pos += 2
    if router:
        rw_ref, rb_ref = refs[pos], refs[pos + 1]
        pos += 2
    h_ref = refs[pos]
    x = x_ref[...]
    xn = x * lax.rsqrt(jnp.mean(x * x, axis=-1, keepdims=True) + NORM_EPS)
    h = xn * nw_ref[...]
    if modulate:
        h = h * (1.0 + sc_ref[...]) + sh_ref[...]
    h_ref[...] = h.astype(h_ref.dtype)
    if router:
        logit_ref = refs[pos + 1]
        logit_ref[...] = jnp.dot(h.astype(BF16), rw_ref[...].astype(BF16), preferred_element_type=F32) + rb_ref[...]


def norm_modulate(x, norm_w, layer, scale=None, shift=None, router_w=None, router_b=None, out_dtype=BF16):
    n, d = x.shape
    tm = ROW_TILE
    modulate = scale is not None
    router = router_w is not None
    in_specs = [pl.BlockSpec((tm, d), lambda i: (i, 0)),
                pl.BlockSpec((None, 1, d), lambda i: (layer, 0, 0))]
    args = [x, norm_w.reshape(norm_w.shape[0], 1, d)]
    if modulate:
        spec = pl.BlockSpec((None, 1, d), lambda i: (_cond_row_of_tile(i, tm), 0, 0))
        in_specs += [spec, spec]
        args += [scale, shift]
    out_shape = [jax.ShapeDtypeStruct((n, d), out_dtype)]
    out_specs = [pl.BlockSpec((tm, d), lambda i: (i, 0))]
    if router:
        in_specs += [pl.BlockSpec((None, d, N_EXPERTS), lambda i: (layer, 0, 0)),
                     pl.BlockSpec((None, 1, N_EXPERTS), lambda i: (layer, 0, 0))]
        args += [router_w, router_b.reshape(router_b.shape[0], 1, N_EXPERTS)]
        out_shape.append(jax.ShapeDtypeStruct((n, N_EXPERTS), F32))
        out_specs.append(pl.BlockSpec((tm, N_EXPERTS), lambda i: (i, 0)))
    res = pl.pallas_call(
        functools.partial(_norm_kernel, modulate=modulate, router=router),
        out_shape=out_shape,
        grid=(n // tm,),
        in_specs=in_specs,
        out_specs=out_specs,
        compiler_params=_params(("arbitrary",)),
        name="norm_modulate",
    )(*args)
    return res if router else res[0]


def _softmax_pv(s, v):
    m = jnp.max(s, axis=-1, keepdims=True)
    p = jnp.exp(s - m)
    p = p / jnp.sum(p, axis=-1, keepdims=True)
    return jnp.dot(p.astype(BF16), v, preferred_element_type=F32)


_NT = (((1,), (1,)), ((), ()))


def _gqa_kernel(q_ref, k_ref, v_ref, o_ref, *, scale):
    k = k_ref[...]
    v = v_ref[...]
    for h in range(GQA_GROUP):
        cols = slice(h * HEAD_DIM, (h + 1) * HEAD_DIM)
        s = lax.dot_general(q_ref[:, cols], k, _NT, preferred_element_type=F32) * scale
        o_ref[:, cols] = _softmax_pv(s, v).astype(o_ref.dtype)


def gqa_attention(q, k, v, tq=256):
    b, s, _ = q.shape
    sk = k.shape[1]
    gw = GQA_GROUP * HEAD_DIM
    return pl.pallas_call(
        functools.partial(_gqa_kernel, scale=HEAD_DIM ** -0.5),
        out_shape=jax.ShapeDtypeStruct(q.shape, BF16),
        grid=(b, GQA_KV_HEADS, s // tq),
        in_specs=[
            pl.BlockSpec((None, tq, gw), lambda bi, j, qi: (bi, qi, j)),
            pl.BlockSpec((None, sk, HEAD_DIM), lambda bi, j, qi: (bi, 0, j)),
            pl.BlockSpec((None, sk, HEAD_DIM), lambda bi, j, qi: (bi, 0, j)),
        ],
        out_specs=pl.BlockSpec((None, tq, gw), lambda bi, j, qi: (bi, qi, j)),
        compiler_params=_params(("arbitrary", "arbitrary", "arbitrary")),
        name="gqa_attention",
    )(q, k, v)


MLA_HEADS_PER_STEP = 2


def _mla_kernel(qn_ref, qp_ref, kv_ref, kpe_ref, o_ref, *, scale):
    kpe = kpe_ref[...]
    for h in range(MLA_HEADS_PER_STEP):
        kcols = slice(h * (MLA_NOPE + MLA_V), h * (MLA_NOPE + MLA_V) + MLA_NOPE)
        vcols = slice(h * (MLA_NOPE + MLA_V) + MLA_NOPE, (h + 1) * (MLA_NOPE + MLA_V))
        s = lax.dot_general(qn_ref[:, h * MLA_NOPE:(h + 1) * MLA_NOPE], kv_ref[:, kcols], _NT,
                            preferred_element_type=F32)
        s = s + lax.dot_general(qp_ref[:, h * MLA_ROPE:(h + 1) * MLA_ROPE], kpe, _NT,
                                preferred_element_type=F32)
        o_ref[:, h * MLA_V:(h + 1) * MLA_V] = _softmax_pv(s * scale, kv_ref[:, vcols]).astype(o_ref.dtype)


def mla_attention(q_nope, q_pe, kv, k_pe, tq=256):
    b, s, _ = q_nope.shape
    sk = kv.shape[1]
    hp = MLA_HEADS_PER_STEP
    return pl.pallas_call(
        functools.partial(_mla_kernel, scale=(MLA_NOPE + MLA_ROPE) ** -0.5),
        out_shape=jax.ShapeDtypeStruct((b, s, MLA_HEADS * MLA_V), BF16),
        grid=(b, MLA_HEADS // hp, s // tq),
        in_specs=[
            pl.BlockSpec((None, tq, hp * MLA_NOPE), lambda bi, g, qi: (bi, qi, g)),
            pl.BlockSpec((None, tq, hp * MLA_ROPE), lambda bi, g, qi: (bi, qi, g)),
            pl.BlockSpec((None, sk, hp * (MLA_NOPE + MLA_V)), lambda bi, g, qi: (bi, 0, g)),
            pl.BlockSpec((None, sk, MLA_ROPE), lambda bi, g, qi: (bi, 0, 0)),
        ],
        out_specs=pl.BlockSpec((None, tq, hp * MLA_V), lambda bi, g, qi: (bi, qi, g)),
        compiler_params=_params(("arbitrary", "arbitrary", "arbitrary")),
        name="mla_attention",
    )(q_nope, q_pe, kv, k_pe)


GDN_VREG_ROWS = 8


def _solve_unit_triangular(a, rhs, reverse):
    n_rows = a.shape[0]
    n_groups = n_rows // GDN_VREG_ROWS
    groups = [rhs[g * GDN_VREG_ROWS:(g + 1) * GDN_VREG_ROWS, :] for g in range(n_groups)]
    for j in (range(n_rows - 1, -1, -1) if reverse else range(n_rows)):
        gj, sj = divmod(j, GDN_VREG_ROWS)
        xj = groups[gj][sj:sj + 1, :]
        for g in (range(gj, -1, -1) if reverse else range(gj, n_groups)):
            col = a[g * GDN_VREG_ROWS:(g + 1) * GDN_VREG_ROWS, j:j + 1]
            groups[g] = groups[g] - col * xj
    return jnp.concatenate(groups, axis=0)


def _gdn_chunk(q, k, v, gc, gc_row, gc_end, beta, state, reverse):
    L = q.shape[0]
    ii = lax.broadcasted_iota(jnp.int32, (L, L), 0)
    jj = lax.broadcasted_iota(jnp.int32, (L, L), 1)
    keep = (ii <= jj) if reverse else (ii >= jj)
    strict = (ii < jj) if reverse else (ii > jj)
    gamma = jnp.where(keep, jnp.exp(jnp.where(keep, gc - gc_row, 0.0)), 0.0)
    kb = k.astype(BF16)
    kk = lax.dot_general(kb, kb, _NT, preferred_element_type=F32)
    qk = lax.dot_general(q.astype(BF16), kb, _NT, preferred_element_type=F32)
    a = jnp.where(strict, kk * gamma * beta, 0.0)
    eg = jnp.exp(gc)
    x = _solve_unit_triangular(a, jnp.concatenate([v * beta, k * (beta * eg)], axis=1), reverse)
    dv = v.shape[1]
    u, w = x[:, :dv], x[:, dv:]
    sb = state.astype(BF16)
    ws = jnp.dot(jnp.concatenate([w, q * eg], axis=0).astype(BF16), sb, preferred_element_type=F32)
    v_new = (u - ws[:L]).astype(BF16)
    o = ws[L:] + jnp.dot((qk * gamma).astype(BF16), v_new, preferred_element_type=F32)
    k_dec = (k * jnp.exp(gc_end - gc)).astype(BF16)
    new_state = state * jnp.exp(gc_end) + lax.dot_general(k_dec, v_new, (((0,), (0,)), ((), ())),
                                                          preferred_element_type=F32)
    return o, new_state


def _gdn_kernel(q_ref, k_ref, v_ref, z_ref, gcol_ref, grow_ref, s0_ref, nw_ref, o_ref, s_out_ref,
                of_scr, ob_scr, st_scr, *, n_chunks):
    L = GDN_CHUNK
    st_scr[...] = s0_ref[...]
    ii = lax.broadcasted_iota(jnp.int32, (L, L), 0)
    jj = lax.broadcasted_iota(jnp.int32, (L, L), 1)
    lower = (ii >= jj).astype(F32)
    upper = (ii <= jj).astype(F32)
    hi = lax.Precision.HIGHEST

    def chunk_step(c, carry):
        for d, reverse in ((0, False), (1, True)):
            cc = (n_chunks - 1 - c) if reverse else c
            r0 = pl.multiple_of(cc * L, L)
            q = q_ref[pl.ds(r0, L), :]
            k = k_ref[pl.ds(r0, L), :]
            v = v_ref[pl.ds(r0, L), :]
            q = q * lax.rsqrt(jnp.sum(q * q, axis=-1, keepdims=True) + NORM_EPS) * (HEAD_DIM ** -0.5)
            k = k * lax.rsqrt(jnp.sum(k * k, axis=-1, keepdims=True) + NORM_EPS)
            gcol = gcol_ref[cc]
            grow = grow_ref[cc]
            csum = jnp.dot(upper if reverse else lower, gcol, precision=hi, preferred_element_type=F32)
            rsum = jnp.dot(grow, lower if reverse else upper, precision=hi, preferred_element_type=F32)
            gc = csum[:, d:d + 1]
            gc_row = rsum[d:d + 1, :]
            gc_end = gc[0:1, :] if reverse else gc[L - 1:L, :]
            o, new_state = _gdn_chunk(q, k, v, gc, gc_row, gc_end, gcol[:, 2 + d:3 + d], st_scr[d], reverse)
            st_scr[d] = new_state
            (ob_scr if reverse else of_scr)[pl.ds(r0, L), :] = o
        return carry

    lax.fori_loop(0, n_chunks, chunk_step, 0)
    s_out_ref[...] = st_scr[...]
    o = of_scr[...] + ob_scr[...]
    o = o * lax.rsqrt(jnp.mean(o * o, axis=-1, keepdims=True) + NORM_EPS) * nw_ref[...]
    z = z_ref[...]
    o_ref[...] = (o * (z * (1.0 / (1.0 + jnp.exp(-z))))).astype(o_ref.dtype)


def gdn_mixer(qkv, proj, z_col0, g, beta, s0, norm_w, n_batch, seq, row0):
    L = GDN_CHUNK
    nc = seq // L
    gb = jnp.concatenate([g, beta], axis=2)
    gcol = jnp.transpose(gb, (0, 3, 1, 2)).reshape(n_batch, GDN_HEADS, nc, L, 4)
    grow = jnp.transpose(g, (0, 3, 2, 1)).reshape(n_batch, GDN_HEADS, 2, nc, L)
    grow = jnp.pad(jnp.swapaxes(grow, 2, 3), ((0, 0), (0, 0), (0, 0), (0, GDN_VREG_ROWS - 2), (0, 0)))
    rb = row0 // seq
    assert row0 % seq == 0 and z_col0 % HEAD_DIM == 0
    zb = z_col0 // HEAD_DIM
    o, s_out = pl.pallas_call(
        functools.partial(_gdn_kernel, n_chunks=nc),
        out_shape=[jax.ShapeDtypeStruct((n_batch * seq, GDN_WIDTH), BF16),
                   jax.ShapeDtypeStruct((n_batch, 2, GDN_HEADS, HEAD_DIM, HEAD_DIM), F32)],
        grid=(n_batch, GDN_HEADS),
        in_specs=[
            pl.BlockSpec((seq, HEAD_DIM), lambda b, h: (rb + b, h)),
            pl.BlockSpec((seq, HEAD_DIM), lambda b, h: (rb + b, GDN_HEADS + h)),
            pl.BlockSpec((seq, HEAD_DIM), lambda b, h: (rb + b, 2 * GDN_HEADS + h)),
            pl.BlockSpec((seq, HEAD_DIM), lambda b, h: (rb + b, zb + h)),
            pl.BlockSpec((None, None, nc, L, 4), lambda b, h: (b, h, 0, 0, 0)),
            pl.BlockSpec((None, None, nc, GDN_VREG_ROWS, L), lambda b, h: (b, h, 0, 0, 0)),
            pl.BlockSpec((None, 2, None, HEAD_DIM, HEAD_DIM), lambda b, h: (b, 0, h, 0, 0)),
            pl.BlockSpec((1, HEAD_DIM), lambda b, h: (0, 0)),
        ],
        out_specs=[
            pl.BlockSpec((seq, HEAD_DIM), lambda b, h: (b, h)),
            pl.BlockSpec((None, 2, None, HEAD_DIM, HEAD_DIM), lambda b, h: (b, 0, h, 0, 0)),
        ],
        scratch_shapes=[pltpu.VMEM((seq, HEAD_DIM), F32), pltpu.VMEM((seq, HEAD_DIM), F32),
                        pltpu.VMEM((2, HEAD_DIM, HEAD_DIM), F32)],
        compiler_params=_params(("arbitrary", "arbitrary")),
        name="gdn_mixer",
    )(qkv, qkv, qkv, proj, gcol, grow, s0, norm_w.reshape(1, HEAD_DIM))
    return o, s_out


SSD_HEADS_PER_GROUP = SSD_HEADS // SSD_GROUPS
SSD_GROUP_WIDTH = SSD_HEADS_PER_GROUP * SSD_HEAD_DIM


def _ssd_head_chunk(xdt, cb, bm, cm, cs, cs_row, cs_end, state, reverse):
    L = xdt.shape[0]
    ii = lax.broadcasted_iota(jnp.int32, (L, L), 0)
    jj = lax.broadcasted_iota(jnp.int32, (L, L), 1)
    keep = (ii <= jj) if reverse else (ii >= jj)
    decay = jnp.where(keep, jnp.exp(jnp.where(keep, cs - cs_row, 0.0)), 0.0)
    y = jnp.dot((cb * decay).astype(BF16), xdt, preferred_element_type=F32)
    y = y + lax.dot_general((cm * jnp.exp(cs)).astype(BF16), state.astype(BF16), _NT, preferred_element_type=F32)
    b_dec = (bm * jnp.exp(cs_end - cs)).astype(BF16)
    new_state = state * jnp.exp(cs_end) + lax.dot_general(xdt, b_dec, (((0,), (0,)), ((), ())),
                                                          preferred_element_type=F32)
    return y, new_state


def _ssd_kernel(x_ref, b_ref, c_ref, dtc_ref, dac_ref, dar_ref, h0_ref, dsk_ref, y_ref, h_out_ref,
                yf_scr, yb_scr, st_scr, *, n_chunks):
    L = SSD_CHUNK
    nh = SSD_HEADS_PER_GROUP
    st_scr[...] = h0_ref[...]
    ii = lax.broadcasted_iota(jnp.int32, (L, L), 0)
    jj = lax.broadcasted_iota(jnp.int32, (L, L), 1)
    lower = (ii >= jj).astype(F32)
    upper = (ii <= jj).astype(F32)
    hi = lax.Precision.HIGHEST

    def chunk_step(c, carry):
        for d, reverse in ((0, False), (1, True)):
            cc = (n_chunks - 1 - c) if reverse else c
            r0 = pl.multiple_of(cc * L, L)
            x = x_ref[pl.ds(r0, L), :]
            bm = b_ref[pl.ds(r0, L), :]
            cm = c_ref[pl.ds(r0, L), :]
            cb = lax.dot_general(cm.astype(BF16), bm.astype(BF16), _NT, preferred_element_type=F32)
            dtc = dtc_ref[cc]
            csum = jnp.dot(upper if reverse else lower, dac_ref[cc], precision=hi, preferred_element_type=F32)
            rsum = jnp.dot(dar_ref[cc], lower if reverse else upper, precision=hi, preferred_element_type=F32)
            ys = []
            for j in range(nh):
                col = d * nh + j
                xdt = (x[:, j * SSD_HEAD_DIM:(j + 1) * SSD_HEAD_DIM] * dtc[:, col:col + 1]).astype(BF16)
                cs = csum[:, col:col + 1]
                cs_end = cs[0:1, :] if reverse else cs[L - 1:L, :]
                y, new_state = _ssd_head_chunk(xdt, cb, bm, cm, cs, rsum[col:col + 1, :], cs_end, st_scr[d, j],
                                               reverse)
                st_scr[d, j] = new_state
                ys.append(y)
            (yb_scr if reverse else yf_scr)[pl.ds(r0, L), :] = jnp.concatenate(ys, axis=1)
        return carry

    lax.fori_loop(0, n_chunks, chunk_step, 0)
    h_out_ref[...] = st_scr[...]
    y_ref[...] = yf_scr[...] + yb_scr[...] + dsk_ref[...] * x_ref[...]


def ssd_mixer(xbc, dt, a, d_skip, h0, n_batch, seq, row0):
    L = SSD_CHUNK
    nc = seq // L
    nh = SSD_HEADS_PER_GROUP

    def group_cols(t):
        t = t.reshape(n_batch, seq, 2, SSD_GROUPS, nh)
        return jnp.transpose(t, (0, 3, 1, 2, 4)).reshape(n_batch, SSD_GROUPS, nc, L, 2 * nh)

    dtc = group_cols(dt)
    dac = group_cols(dt * a)
    dar = jnp.swapaxes(dac, 3, 4)
    dsk = jnp.repeat(jnp.sum(d_skip, axis=0), SSD_HEAD_DIM).reshape(1, SSD_INNER)
    rb = row0 // seq
    assert row0 % seq == 0
    gw = SSD_GROUP_WIDTH
    b_blk0 = SSD_INNER // SSD_STATE
    c_blk0 = b_blk0 + SSD_GROUPS
    y, h_out = pl.pallas_call(
        functools.partial(_ssd_kernel, n_chunks=nc),
        out_shape=[jax.ShapeDtypeStruct((n_batch * seq, SSD_INNER), F32),
                   jax.ShapeDtypeStruct((n_batch, 2, SSD_HEADS, SSD_HEAD_DIM, SSD_STATE), F32)],
        grid=(n_batch, SSD_GROUPS),
        in_specs=[
            pl.BlockSpec((seq, gw), lambda b, g: (rb + b, g)),
            pl.BlockSpec((seq, SSD_STATE), lambda b, g: (rb + b, b_blk0 + g)),
            pl.BlockSpec((seq, SSD_STATE), lambda b, g: (rb + b, c_blk0 + g)),
            pl.BlockSpec((None, None, nc, L, 2 * nh), lambda b, g: (b, g, 0, 0, 0)),
            pl.BlockSpec((None, None, nc, L, 2 * nh), lambda b, g: (b, g, 0, 0, 0)),
            pl.BlockSpec((None, None, nc, 2 * nh, L), lambda b, g: (b, g, 0, 0, 0)),
            pl.BlockSpec((None, 2, nh, SSD_HEAD_DIM, SSD_STATE), lambda b, g: (b, 0, g, 0, 0)),
            pl.BlockSpec((1, gw), lambda b, g: (0, g)),
        ],
        out_specs=[
            pl.BlockSpec((seq, gw), lambda b, g: (b, g)),
            pl.BlockSpec((None, 2, nh, SSD_HEAD_DIM, SSD_STATE), lambda b, g: (b, 0, g, 0, 0)),
        ],
        scratch_shapes=[pltpu.VMEM((seq, gw), F32), pltpu.VMEM((seq, gw), F32),
                        pltpu.VMEM((2, nh, SSD_HEAD_DIM, SSD_STATE), F32)],
        compiler_params=_params(("arbitrary", "arbitrary")),
        name="ssd_mixer",
    )(xbc, xbc, xbc, dtc, dac, dar, h0, dsk)
    return y, h_out


def _gated_norm_kernel(y_ref, z0_ref, z1_ref, w_ref, o_ref):
    z = jnp.concatenate([z0_ref[...], z1_ref[...]], axis=1)
    t = y_ref[...] * (z * (1.0 / (1.0 + jnp.exp(-z))))
    t = t * lax.rsqrt(jnp.mean(t * t, axis=-1, keepdims=True) + NORM_EPS) * w_ref[...]
    o_ref[...] = t.astype(o_ref.dtype)


def ssd_gated_norm(y, proj, z_col0, norm_w):
    n, d = y.shape
    tm = ROW_TILE
    half = d // 2
    assert z_col0 % half == 0
    zb = z_col0 // half
    return pl.pallas_call(
        _gated_norm_kernel,
        out_shape=jax.ShapeDtypeStruct((n, d), BF16),
        grid=(n // tm,),
        in_specs=[pl.BlockSpec((tm, d), lambda i: (i, 0)),
                  pl.BlockSpec((tm, half), lambda i: (i, zb)),
                  pl.BlockSpec((tm, half), lambda i: (i, zb + 1)),
                  pl.BlockSpec((1, d), lambda i: (0, 0))],
        out_specs=pl.BlockSpec((tm, d), lambda i: (i, 0)),
        compiler_params=_params(("arbitrary",)),
        name="ssd_gated_norm",
    )(y, proj, proj, norm_w.reshape(1, d))


ITEM_VALID = 1
ITEM_NEW_WEIGHTS = 2


def _moe_up_kernel(ib_ref, ic_ref, iwe_ref, iwc_ref, iflag_ref, x_ref, wg_ref, wu_ref, bg_ref, bu_ref, o_ref,
                   wg_bf, wu_bf):
    flag = iflag_ref[pl.program_id(0)]

    @pl.when((flag & ITEM_NEW_WEIGHTS) != 0)
    def _():
        wg_bf[...] = wg_ref[...].astype(BF16)
        wu_bf[...] = wu_ref[...].astype(BF16)

    @pl.when((flag & ITEM_VALID) != 0)
    def _():
        x = x_ref[...]
        gate = jnp.minimum(jnp.dot(x, wg_bf[...], preferred_element_type=F32) + bg_ref[...], SWIGLU_LIMIT)
        up = jnp.clip(jnp.dot(x, wu_bf[...], preferred_element_type=F32) + bu_ref[...], -SWIGLU_LIMIT, SWIGLU_LIMIT)
        act = (up + 1.0) * gate * (1.0 / (1.0 + jnp.exp(-SWIGLU_ALPHA * gate)))
        o_ref[...] = act.astype(o_ref.dtype)

    @pl.when((flag & ITEM_VALID) == 0)
    def _():
        o_ref[...] = jnp.zeros_like(o_ref)


def _moe_down_kernel(ib_ref, ic_ref, iwe_ref, iwc_ref, iflag_ref, a_ref, wd_ref, bd_ref, g_ref, o_ref, wd_bf):
    flag = iflag_ref[pl.program_id(0)]

    @pl.when((flag & ITEM_NEW_WEIGHTS) != 0)
    def _():
        wd_bf[...] = wd_ref[...].astype(BF16)

    @pl.when((flag & ITEM_VALID) != 0)
    def _():
        y = jnp.dot(a_ref[...], wd_bf[...], preferred_element_type=F32) + bd_ref[...]
        o_ref[...] = y * g_ref[...]

    @pl.when((flag & ITEM_VALID) == 0)
    def _():
        o_ref[...] = jnp.zeros_like(o_ref)


def _moe_items(blk_start, nblk, n_used, n_chunks):
    n_items = MOE_NBLK * n_chunks
    i = jnp.arange(n_items, dtype=jnp.int32)
    item_end = (blk_start + nblk) * n_chunks
    e = jnp.minimum(jnp.searchsorted(item_end, i, side='right'), N_EXPERTS - 1).astype(jnp.int32)
    r = i - blk_start[e] * n_chunks
    nb = jnp.maximum(nblk[e], 1)
    valid = i < n_used * n_chunks
    n_tail = jnp.maximum(MOE_NBLK - n_used, 1)
    rt = i - n_used * n_chunks
    chunk = jnp.where(valid, r // nb, rt // n_tail).astype(jnp.int32)
    blk = jnp.where(valid, blk_start[e] + r % nb, n_used + rt % n_tail).astype(jnp.int32)
    last = jnp.maximum(n_used * n_chunks - 1, 0)
    we = jnp.where(valid, e, e[last]).astype(jnp.int32)
    wc = jnp.where(valid, chunk, chunk[last]).astype(jnp.int32)
    new_w = jnp.concatenate([jnp.ones((1,), bool), (we[1:] != we[:-1]) | (wc[1:] != wc[:-1])])
    flag = valid.astype(jnp.int32) * ITEM_VALID + new_w.astype(jnp.int32) * ITEM_NEW_WEIGHTS
    return blk, chunk, we, wc, flag


def moe_ffn(h, logits, layer, w_gate, b_gate, w_up, b_up, w_down, b_down):
    t, d = h.shape
    n_assign = t * TOP_K
    top_logit, top_idx = lax.top_k(logits, TOP_K)
    gates = jax.nn.softmax(top_logit, axis=-1)
    flat_e = top_idx.reshape(n_assign).astype(jnp.int32)
    order = jnp.argsort(flat_e).astype(jnp.int32)
    sorted_e = flat_e[order]
    counts = jnp.bincount(flat_e, length=N_EXPERTS).astype(jnp.int32)
    nblk = (counts + MOE_TM - 1) // MOE_TM
    blk_end = jnp.cumsum(nblk).astype(jnp.int32)
    blk_start = blk_end - nblk
    n_used = blk_end[-1]
    sort_start = jnp.cumsum(counts).astype(jnp.int32) - counts
    dest = blk_start[sorted_e] * MOE_TM + jnp.arange(n_assign, dtype=jnp.int32) - sort_start[sorted_e]
    slot_tok = jnp.full((MOE_SLOTS,), t, jnp.int32).at[dest].set(order // TOP_K)
    slot_g = jnp.zeros((MOE_SLOTS,), F32).at[dest].set(gates.reshape(n_assign)[order])
    slot_of_assign = jnp.zeros((n_assign,), jnp.int32).at[order].set(dest).reshape(t, TOP_K)
    x_slots = jnp.concatenate([h, jnp.zeros((1, d), h.dtype)], axis=0)[slot_tok]

    depth = w_gate.shape[0]
    n_fc = EXPERT_FF // MOE_TF
    items = _moe_items(blk_start, nblk, n_used, n_fc)
    act = pl.pallas_call(
        _moe_up_kernel,
        out_shape=jax.ShapeDtypeStruct((MOE_SLOTS, EXPERT_FF), BF16),
        grid_spec=pltpu.PrefetchScalarGridSpec(
            num_scalar_prefetch=5,
            grid=(MOE_NBLK * n_fc,),
            in_specs=[
                pl.BlockSpec((MOE_TM, d), lambda i, ib, ic, iwe, iwc, ifl: (ib[i], 0)),
                pl.BlockSpec((None, None, d, MOE_TF), lambda i, ib, ic, iwe, iwc, ifl: (layer, iwe[i], 0, iwc[i])),
                pl.BlockSpec((None, None, d, MOE_TF), lambda i, ib, ic, iwe, iwc, ifl: (layer, iwe[i], 0, iwc[i])),
                pl.BlockSpec((None, None, 1, MOE_TF), lambda i, ib, ic, iwe, iwc, ifl: (layer, iwe[i], 0, iwc[i])),
                pl.BlockSpec((None, None, 1, MOE_TF), lambda i, ib, ic, iwe, iwc, ifl: (layer, iwe[i], 0, iwc[i])),
            ],
            out_specs=pl.BlockSpec((MOE_TM, MOE_TF), lambda i, ib, ic, iwe, iwc, ifl: (ib[i], ic[i])),
            scratch_shapes=[pltpu.VMEM((d, MOE_TF), BF16), pltpu.VMEM((d, MOE_TF), BF16)],
        ),
        compiler_params=_params(("arbitrary",)),
        name="moe_gate_up",
    )(*items, x_slots, w_gate, w_up,
      b_gate.reshape(depth, N_EXPERTS, 1, EXPERT_FF), b_up.reshape(depth, N_EXPERTS, 1, EXPERT_FF))

    n_nc = d // MOE_TN
    items = _moe_items(blk_start, nblk, n_used, n_nc)
    y = pl.pallas_call(
        _moe_down_kernel,
        out_shape=jax.ShapeDtypeStruct((MOE_SLOTS, d), F32),
        grid_spec=pltpu.PrefetchScalarGridSpec(
            num_scalar_prefetch=5,
            grid=(MOE_NBLK * n_nc,),
            in_specs=[
                pl.BlockSpec((MOE_TM, EXPERT_FF), lambda i, ib, ic, iwe, iwc, ifl: (ib[i], 0)),
                pl.BlockSpec((None, None, EXPERT_FF, MOE_TN),
                             lambda i, ib, ic, iwe, iwc, ifl: (layer, iwe[i], 0, iwc[i])),
                pl.BlockSpec((None, None, 1, MOE_TN), lambda i, ib, ic, iwe, iwc, ifl: (layer, iwe[i], 0, iwc[i])),
                pl.BlockSpec((MOE_TM, 1), lambda i, ib, ic, iwe, iwc, ifl: (ib[i], 0)),
            ],
            out_specs=pl.BlockSpec((MOE_TM, MOE_TN), lambda i, ib, ic, iwe, iwc, ifl: (ib[i], ic[i])),
            scratch_shapes=[pltpu.VMEM((EXPERT_FF, MOE_TN), BF16)],
        ),
        compiler_params=_params(("arbitrary",)),
        name="moe_down",
    )(*items, act, w_down, b_down.reshape(depth, N_EXPERTS, 1, d), slot_g.reshape(MOE_SLOTS, 1))
    return jnp.sum(y[slot_of_assign], axis=1)


def _split(x, sizes):
    return jnp.split(x, np.cumsum(sizes)[:-1].tolist(), axis=-1)


def _rms(x, w):
    return (x * lax.rsqrt(jnp.mean(x * x, axis=-1, keepdims=True) + NORM_EPS)) * w


def _rope_1d(x, pos):
    half = x.shape[-1] // 2
    inv_freq = ROPE_BASE ** (-jnp.arange(half, dtype=F32) / half)
    ang = pos.astype(F32)[:, None] * inv_freq
    cos = jnp.cos(ang)[:, None, :]
    sin = jnp.sin(ang)[:, None, :]
    x1, x2 = x[..., :half], x[..., half:]
    return jnp.concatenate([x1 * cos - x2 * sin, x2 * cos + x1 * sin], axis=-1)


def _rope_2d(x, n_rows):
    t = jnp.arange(n_rows * GRID_W)
    d_axis = x.shape[-1] // 2
    return jnp.concatenate([_rope_1d(x[..., :d_axis], t // GRID_W), _rope_1d(x[..., d_axis:], t % GRID_W)], axis=-1)


def _short_conv(x, w):
    pad = (w.shape[0] - 1) // 2
    s = x.shape[1]
    xp = jnp.pad(x, ((0, 0), (pad, pad), (0, 0)))
    out = xp[:, 0:s] * w[0]
    for j in range(1, w.shape[0]):
        out = out + xp[:, j:j + s] * w[j]
    return out


def _conv_silu_merged(cols, conv_w, conv_b=None):
    outs = []
    for t in (cols[:N_PROMPT].reshape(BATCH, SEQ, -1), cols[N_PROMPT:].reshape(DEC_BATCH, DEC_SEQ, -1)):
        t = _short_conv(t, conv_w)
        if conv_b is not None:
            t = t + conv_b
        outs.append(jax.nn.silu(t).reshape(-1, t.shape[-1]))
    return jnp.concatenate(outs, axis=0)


def _even_attention_pass(proj, q_norm, k_norm, ctx, n_rows):
    b, s, _ = proj.shape
    q, k, v = _split(proj[..., :sum(EVEN_SPLITS[:3])], EVEN_SPLITS[:3])
    q = _rms(q.reshape(b, s, GQA_HEADS, HEAD_DIM), q_norm)
    k = _rms(k.reshape(b, s, GQA_KV_HEADS, HEAD_DIM), k_norm)
    v = v.reshape(b, s, GQA_KV_HEADS, HEAD_DIM)
    if ctx is None:
        k_all, v_all = k, v
    else:
        k_ctx, v_ctx = ctx
        q = _rope_2d(q, n_rows)
        k_all = jnp.concatenate([k_ctx, _rope_2d(k, n_rows)], axis=1)
        v_all = jnp.concatenate([v_ctx, v], axis=1)
    sk = k_all.shape[1]
    attn = gqa_attention(q.reshape(b, s, -1).astype(BF16), k_all.reshape(b, sk, -1).astype(BF16),
                         v_all.reshape(b, sk, -1).astype(BF16))
    return attn, k, v


def kernel(x_prompt, x_sample, cache_gqa_k, cache_gqa_v, state_ssd, state_gdn, cache_mla_ckv, cache_mla_kpe, c, c_ctx, w_mod, b_mod, norm_mix, norm_ffn, ev_w_in, ev_w_out, gqa_q_norm, gqa_k_norm, ssd_conv_w, ssd_conv_b, ssd_a_log, ssd_dt_bias, ssd_d_skip, ssd_norm, od_w_in, od_w_out, gdn_conv_w, gdn_a_log, gdn_dt_bias, gdn_norm, mla_q_a_norm, mla_w_q_b, mla_kv_a_norm, mla_w_kv_b, router_w, router_b, moe_w_gate, moe_b_gate, moe_w_up, moe_b_up, moe_w_down, moe_b_down, final_norm):
    n_rows = DEC_SEQ // GRID_W
    x = jnp.concatenate([x_prompt.reshape(N_PROMPT, D_MODEL), x_sample.reshape(N_SAMPLE, D_MODEL)], axis=0)

    cond = jnp.concatenate([c_ctx[None, :], c, jnp.zeros((N_COND_PAD - N_COND, D_MODEL), F32)], axis=0)
    mods = modulation(jax.nn.silu(cond).astype(BF16), w_mod, b_mod)
    mods = mods.reshape(DEPTH, N_COND_PAD, N_MOD, 1, D_MODEL)

    def split_passes(t):
        return (t[:N_PROMPT].reshape(BATCH, SEQ, -1), t[N_PROMPT:].reshape(DEC_BATCH, DEC_SEQ, -1))

    def merge_passes(tp, ts):
        return jnp.concatenate([tp.reshape(N_PROMPT, -1), ts.reshape(N_SAMPLE, -1)], axis=0)

    ctx_even, ctx_odd = [], []
    for l in range(DEPTH):
        i = l // 2
        sh1, sc1, g1, sh2, sc2, g2 = (mods[l, :, j] for j in range(N_MOD))
        h = norm_modulate(x, norm_mix, l, sc1, sh1)
        if l % 2 == 0:
            proj = matmul(h, ev_w_in, (i,), n_cols=EVEN_MAIN, name="even_in_proj")
            dt_proj = matmul(h, ev_w_in[i, :, EVEN_MAIN:], name="even_dt_proj")
            pp, ps = split_passes(proj)
            attn_p, k_p, v_p = _even_attention_pass(pp, gqa_q_norm[i], gqa_k_norm[i], None, None)
            attn_s, _, _ = _even_attention_pass(ps, gqa_q_norm[i], gqa_k_norm[i],
                                                (cache_gqa_k[:, i], cache_gqa_v[:, i]), n_rows)
            xbc = _conv_silu_merged(proj[:, EVEN_MAIN - SSD_CONV_CH:], ssd_conv_w[i], ssd_conv_b[i])
            dt = jax.nn.softplus(dt_proj.reshape(N_TOK, 2, SSD_HEADS) + ssd_dt_bias[i])
            dt_p, dt_s = dt[:N_PROMPT].reshape(BATCH, SEQ, 2, SSD_HEADS), dt[N_PROMPT:].reshape(
                DEC_BATCH, DEC_SEQ, 2, SSD_HEADS)
            a = -jnp.exp(ssd_a_log[i])
            y_p, h_p = ssd_mixer(xbc, dt_p, a, ssd_d_skip[i],
                                 jnp.zeros((BATCH, 2, SSD_HEADS, SSD_HEAD_DIM, SSD_STATE), F32), BATCH, SEQ, 0)
            y_s, _ = ssd_mixer(xbc, dt_s, a, ssd_d_skip[i], state_ssd[:, i], DEC_BATCH, DEC_SEQ, N_PROMPT)
            yn = ssd_gated_norm(jnp.concatenate([y_p, y_s], axis=0), proj, sum(EVEN_SPLITS[:3]), ssd_norm[i])
            ctx_even.append((k_p, v_p, h_p))
            feat = jnp.concatenate([merge_passes(attn_p, attn_s), yn], axis=-1)
            x = matmul(feat, ev_w_out, (i,), residual=x, gate=g1, name="even_out_proj")
        else:
            proj = matmul(h, od_w_in, (i,), tn=896, name="odd_in_proj")
            off = np.cumsum((0,) + ODD_SPLITS)
            qkv = _conv_silu_merged(proj[:, :off[1]], gdn_conv_w[i])
            gates = proj[:, off[2]:off[3]].reshape(N_TOK, 2, 2, GDN_HEADS)
            g = -jnp.exp(gdn_a_log[i]) * jax.nn.softplus(gates[:, 0] + gdn_dt_bias[i])
            beta = jax.nn.sigmoid(gates[:, 1])

            def gate_passes(t):
                return (t[:N_PROMPT].reshape(BATCH, SEQ, 2, GDN_HEADS),
                        t[N_PROMPT:].reshape(DEC_BATCH, DEC_SEQ, 2, GDN_HEADS))

            (g_p, g_s), (beta_p, beta_s) = gate_passes(g), gate_passes(beta)
            o_p, s_p = gdn_mixer(qkv, proj, int(off[1]), g_p, beta_p,
                                 jnp.zeros((BATCH, 2, GDN_HEADS, HEAD_DIM, HEAD_DIM), F32), gdn_norm[i], BATCH, SEQ, 0)
            o_s, _ = gdn_mixer(qkv, proj, int(off[1]), g_s, beta_s, state_gdn[:, i], gdn_norm[i], DEC_BATCH, DEC_SEQ,
                               N_PROMPT)
            o_p = o_p.reshape(BATCH, SEQ, -1)
            o_s = o_s.reshape(DEC_BATCH, DEC_SEQ, -1)
            qa = _rms(proj[:, off[3]:off[4]], mla_q_a_norm[i]).astype(BF16)
            ckv = _rms(proj[:, off[4]:off[5]], mla_kv_a_norm[i])
            ckv_p, ckv_s = split_passes(ckv)
            kpe_p, kpe_s = split_passes(proj[:, off[5]:off[6]])
            ctx_odd.append((s_p, ckv_p, kpe_p))
            qm = matmul(qa, mla_w_q_b, (i,), name="mla_q_b")
            ckv_all_s = jnp.concatenate([cache_mla_ckv[:, i], ckv_s], axis=1)
            sk_s = PAST_LEN + DEC_SEQ
            kv = matmul(jnp.concatenate([ckv_p.reshape(N_PROMPT, -1), ckv_all_s.reshape(DEC_BATCH * sk_s, -1)],
                                        axis=0).astype(BF16),
                        mla_w_kv_b, (i,), out_dtype=BF16, name="mla_kv_b")
            kv_p = kv[:N_PROMPT].reshape(BATCH, SEQ, -1)
            kv_s = kv[N_PROMPT:].reshape(DEC_BATCH, sk_s, -1)
            qm_p, qm_s = split_passes(qm)
            qm_p = qm_p.reshape(BATCH, SEQ, MLA_HEADS, MLA_NOPE + MLA_ROPE)
            qm_s = qm_s.reshape(DEC_BATCH, DEC_SEQ, MLA_HEADS, MLA_NOPE + MLA_ROPE)
            qpe_s = _rope_2d(qm_s[..., MLA_NOPE:], n_rows)
            kpe_all_s = jnp.concatenate([cache_mla_kpe[:, i], _rope_2d(kpe_s[:, :, None, :], n_rows)[:, :, 0]],
                                        axis=1)
            om_p = mla_attention(qm_p[..., :MLA_NOPE].reshape(BATCH, SEQ, -1).astype(BF16),
                                 qm_p[..., MLA_NOPE:].reshape(BATCH, SEQ, -1).astype(BF16), kv_p,
                                 kpe_p.astype(BF16))
            om_s = mla_attention(qm_s[..., :MLA_NOPE].reshape(DEC_BATCH, DEC_SEQ, -1).astype(BF16),
                                 qpe_s.reshape(DEC_BATCH, DEC_SEQ, -1).astype(BF16), kv_s, kpe_all_s.astype(BF16))
            feat = merge_passes(jnp.concatenate([o_p, om_p], axis=-1), jnp.concatenate([o_s, om_s], axis=-1))
            x = matmul(feat, od_w_out, (i,), residual=x, gate=g1, name="odd_out_proj")
        h, logits = norm_modulate(x, norm_ffn, l, sc2, sh2, router_w, router_b)
        ffn = moe_ffn(h, logits, l, moe_w_gate, moe_b_gate, moe_w_up, moe_b_up, moe_w_down, moe_b_down)
        gate_rows = jnp.concatenate([jnp.broadcast_to(g2[0], (N_PROMPT, D_MODEL)),
                                     jnp.repeat(g2[1:N_COND, 0], DEC_SEQ, axis=0)], axis=0)
        x = x + gate_rows * ffn

    y = norm_modulate(x, final_norm[None, :], 0, out_dtype=F32)
    y_prompt = y[:N_PROMPT].reshape(BATCH, SEQ, D_MODEL)
    y_sample = y[N_PROMPT:].reshape(DEC_BATCH, DEC_SEQ, D_MODEL)
    new_gqa_k = jnp.stack([t[0] for t in ctx_even], axis=1)
    new_gqa_v = jnp.stack([t[1] for t in ctx_even], axis=1)
    new_ssd = jnp.stack([t[2] for t in ctx_even], axis=1)
    new_gdn = jnp.stack([t[0] for t in ctx_odd], axis=1)
    new_mla_ckv = jnp.stack([t[1] for t in ctx_odd], axis=1)
    new_mla_kpe = jnp.stack([t[2] for t in ctx_odd], axis=1)
    return (y_prompt, y_sample, new_gqa_k, new_gqa_v, new_ssd, new_gdn, new_mla_ckv, new_mla_kpe)
```

```python
import functools
import math

import jax
import jax.numpy as jnp
import numpy as np
from jax import lax
from jax.experimental import pallas as pl
from jax.experimental.pallas import tpu as pltpu

F32 = jnp.float32
BF16 = jnp.bfloat16

D_MODEL = 4096
BATCH = 16
SEQ = 256
DEPTH = 4
DEC_BATCH = 8
DEC_SEQ = 1024
PAST_LEN = 256
GRID_W = 64
HEAD_DIM = 128
ROPE_BASE = 10000.0
NORM_EPS = 1e-6
CONV_WIDTH = 5
GROUP_WIDTH = D_MODEL // 2
N_MOD = 6
GQA_HEADS = GROUP_WIDTH // HEAD_DIM
GQA_KV_HEADS = GQA_HEADS // 4
GQA_GROUP = GQA_HEADS // GQA_KV_HEADS
SSD_INNER = GROUP_WIDTH
SSD_HEAD_DIM = 64
SSD_HEADS = SSD_INNER // SSD_HEAD_DIM
SSD_GROUPS = 4
SSD_STATE = 128
SSD_CHUNK = 128
SSD_CONV_CH = SSD_INNER + 2 * SSD_GROUPS * SSD_STATE
GDN_HEADS = GROUP_WIDTH // HEAD_DIM
GDN_WIDTH = GDN_HEADS * HEAD_DIM
GDN_CHUNK = 64
MLA_HEADS = GROUP_WIDTH // HEAD_DIM
MLA_Q_LORA = D_MODEL // 4
MLA_KV_LORA = 512
MLA_NOPE = 128
MLA_ROPE = 64
MLA_V = 128
N_EXPERTS = 32
TOP_K = 4
EXPERT_FF = 1024
SWIGLU_LIMIT = 7.0
SWIGLU_ALPHA = 1.702
EVEN_SPLITS = (GQA_HEADS * HEAD_DIM, GQA_KV_HEADS * HEAD_DIM, GQA_KV_HEADS * HEAD_DIM, SSD_INNER, SSD_CONV_CH,
               2 * SSD_HEADS)
ODD_SPLITS = (3 * GDN_WIDTH, GDN_WIDTH, 4 * GDN_HEADS, MLA_Q_LORA, MLA_KV_LORA, MLA_ROPE)
EVEN_MAIN = sum(EVEN_SPLITS[:-1])

N_PROMPT = BATCH * SEQ
N_SAMPLE = DEC_BATCH * DEC_SEQ
N_TOK = N_PROMPT + N_SAMPLE
N_COND = 1 + DEC_BATCH
N_COND_PAD = 16

V7X_VMEM_LIMIT_BYTES = 56 * 1024 * 1024

ROW_TILE = 256
MOE_TM = 256
MOE_TF = 512
MOE_TN = 2048
MOE_NBLK = N_TOK * TOP_K // MOE_TM + N_EXPERTS
MOE_SLOTS = MOE_NBLK * MOE_TM


def _cond_row_of_tile(tile, rows_per_tile):
    row0 = tile * rows_per_tile
    return jnp.where(row0 < N_PROMPT, 0, 1 + (row0 - N_PROMPT) // DEC_SEQ)


def _params(semantics):
    return pltpu.CompilerParams(dimension_semantics=semantics, vmem_limit_bytes=V7X_VMEM_LIMIT_BYTES)


def _mm_kernel(*refs, has_bias, has_res):
    a_ref, w_ref = refs[0], refs[1]
    pos = 2
    bias_ref = res_ref = gate_ref = None
    if has_bias:
        bias_ref = refs[pos]
        pos += 1
    if has_res:
        res_ref, gate_ref = refs[pos], refs[pos + 1]
        pos += 2
    o_ref, wbf_ref = refs[pos], refs[pos + 1]

    @pl.when(pl.program_id(1) == 0)
    def _():
        wbf_ref[...] = w_ref[...].astype(BF16)

    acc = jnp.dot(a_ref[...], wbf_ref[...], preferred_element_type=F32)
    if has_bias:
        acc = acc + bias_ref[...]
    if has_res:
        acc = res_ref[...] + gate_ref[...] * acc
    o_ref[...] = acc.astype(o_ref.dtype)


def matmul(a, w, prefix=(), *, n_cols=None, bias=None, residual=None, gate=None, out_dtype=F32, tm=512, tn=512,
           name="mm"):
    m, k = a.shape
    n = w.shape[-1] if n_cols is None else n_cols
    assert w.shape[-2] == k and len(w.shape) == len(prefix) + 2
    tm = min(tm, m)
    tn = min(tn, n)
    assert m % tm == 0 and n % tn == 0
    grid = (n // tn, m // tm)
    npre = len(prefix)
    in_specs = [
        pl.BlockSpec((tm, k), lambda j, i: (i, 0)),
        pl.BlockSpec((None,) * npre + (k, tn), lambda j, i: prefix + (0, j)),
    ]
    args = [a, w]
    if bias is not None:
        b2 = bias.reshape(bias.shape[:-1] + (1, bias.shape[-1]))
        in_specs.append(pl.BlockSpec((None,) * npre + (1, tn), lambda j, i: prefix + (0, j)))
        args.append(b2)
    if residual is not None:
        in_specs.append(pl.BlockSpec((tm, tn), lambda j, i: (i, j)))
        in_specs.append(pl.BlockSpec((None, 1, tn), lambda j, i: (_cond_row_of_tile(i, tm), 0, j)))
        args += [residual, gate]
    return pl.pallas_call(
        functools.partial(_mm_kernel, has_bias=bias is not None, has_res=residual is not None),
        out_shape=jax.ShapeDtypeStruct((m, n), out_dtype),
        grid=grid,
        in_specs=in_specs,
        out_specs=pl.BlockSpec((tm, tn), lambda j, i: (i, j)),
        scratch_shapes=[pltpu.VMEM((k, tn), BF16)],
        compiler_params=_params(("arbitrary", "arbitrary")),
        name=name,
    )(*args)


def _mod_kernel(a_ref, w_ref, b_ref, o_ref):
    acc = jnp.dot(a_ref[...], w_ref[...].astype(BF16), preferred_element_type=F32)
    o_ref[...] = acc + b_ref[...]


def modulation(cond_act, w_mod, b_mod, tn=512):
    depth, k, n = w_mod.shape
    return pl.pallas_call(
        _mod_kernel,
        out_shape=jax.ShapeDtypeStruct((depth, N_COND_PAD, n), F32),
        grid=(depth, n // tn),
        in_specs=[
            pl.BlockSpec((N_COND_PAD, k), lambda l, j: (0, 0)),
            pl.BlockSpec((None, k, tn), lambda l, j: (l, 0, j)),
            pl.BlockSpec((None, 1, tn), lambda l, j: (l, 0, j)),
        ],
        out_specs=pl.BlockSpec((None, N_COND_PAD, tn), lambda l, j: (l, 0, j)),
        compiler_params=_params(("arbitrary", "arbitrary")),
        name="adaln_modulation",
    )(cond_act, w_mod, b_mod.reshape(depth, 1, n))


def _norm_kernel(*refs, modulate, router):
    x_ref, nw_ref = refs[0], refs[1]
    pos = 2
    if modulate:
        sc_ref, sh_ref = refs[pos], refs[pos + 1]
        pos += 2
    if router:
        rw_ref, rb_ref = refs[pos], refs[pos + 1]
        pos += 2
    h_ref = refs[pos]
    x = x_ref[...]
    xn = x * lax.rsqrt(jnp.mean(x * x, axis=-1, keepdims=True) + NORM_EPS)
    h = xn * nw_ref[...]
    if modulate:
        h = h * (1.0 + sc_ref[...]) + sh_ref[...]
    h_ref[...] = h.astype(h_ref.dtype)
    if router:
        logit_ref = refs[pos + 1]
        logit_ref[...] = jnp.dot(h.astype(BF16), rw_ref[...].astype(BF16), preferred_element_type=F32) + rb_ref[...]


def norm_modulate(x, norm_w, layer, scale=None, shift=None, router_w=None, router_b=None, out_dtype=BF16):
    n, d = x.shape
    tm = ROW_TILE
    modulate = scale is not None
    router = router_w is not None
    in_specs = [pl.BlockSpec((tm, d), lambda i: (i, 0)),
                pl.BlockSpec((None, 1, d), lambda i: (layer, 0, 0))]
    args = [x, norm_w.reshape(norm_w.shape[0], 1, d)]
    if modulate:
        spec = pl.BlockSpec((None, 1, d), lambda i: (_cond_row_of_tile(i, tm), 0, 0))
        in_specs += [spec, spec]
        args += [scale, shift]
    out_shape = [jax.ShapeDtypeStruct((n, d), out_dtype)]
    out_specs = [pl.BlockSpec((tm, d), lambda i: (i, 0))]
    if router:
        in_specs += [pl.BlockSpec((None, d, N_EXPERTS), lambda i: (layer, 0, 0)),
                     pl.BlockSpec((None, 1, N_EXPERTS), lambda i: (layer, 0, 0))]
        args += [router_w, router_b.reshape(router_b.shape[0], 1, N_EXPERTS)]
        out_shape.append(jax.ShapeDtypeStruct((n, N_EXPERTS), F32))
        out_specs.append(pl.BlockSpec((tm, N_EXPERTS), lambda i: (i, 0)))
    res = pl.pallas_call(
        functools.partial(_norm_kernel, modulate=modulate, router=router),
        out_shape=out_shape,
        grid=(n // tm,),
        in_specs=in_specs,
        out_specs=out_specs,
        compiler_params=_params(("arbitrary",)),
        name="norm_modulate",
    )(*args)
    return res if router else res[0]


def _softmax_pv(s, v):
    m = jnp.max(s, axis=-1, keepdims=True)
    p = jnp.exp(s - m)
    p = p / jnp.sum(p, axis=-1, keepdims=True)
    return jnp.dot(p.astype(BF16), v, preferred_element_type=F32)


_NT = (((1,), (1,)), ((), ()))


def _gqa_kernel(q_ref, k_ref, v_ref, o_ref, *, scale):
    k = k_ref[...]
    v = v_ref[...]
    for h in range(GQA_GROUP):
        cols = slice(h * HEAD_DIM, (h + 1) * HEAD_DIM)
        s = lax.dot_general(q_ref[:, cols], k, _NT, preferred_element_type=F32) * scale
        o_ref[:, cols] = _softmax_pv(s, v).astype(o_ref.dtype)


def gqa_attention(q, k, v, tq=256):
    b, s, _ = q.shape
    sk = k.shape[1]
    gw = GQA_GROUP * HEAD_DIM
    return pl.pallas_call(
        functools.partial(_gqa_kernel, scale=HEAD_DIM ** -0.5),
        out_shape=jax.ShapeDtypeStruct(q.shape, BF16),
        grid=(b, GQA_KV_HEADS, s // tq),
        in_specs=[
            pl.BlockSpec((None, tq, gw), lambda bi, j, qi: (bi, qi, j)),
            pl.BlockSpec((None, sk, HEAD_DIM), lambda bi, j, qi: (bi, 0, j)),
            pl.BlockSpec((None, sk, HEAD_DIM), lambda bi, j, qi: (bi, 0, j)),
        ],
        out_specs=pl.BlockSpec((None, tq, gw), lambda bi, j, qi: (bi, qi, j)),
        compiler_params=_params(("arbitrary", "arbitrary", "arbitrary")),
        name="gqa_attention",
    )(q, k, v)


MLA_HEADS_PER_STEP = 2


def _mla_kernel(qn_ref, qp_ref, kv_ref, kpe_ref, o_ref, *, scale):
    kpe = kpe_ref[...]
    for h in range(MLA_HEADS_PER_STEP):
        kcols = slice(h * (MLA_NOPE + MLA_V), h * (MLA_NOPE + MLA_V) + MLA_NOPE)
        vcols = slice(h * (MLA_NOPE + MLA_V) + MLA_NOPE, (h + 1) * (MLA_NOPE + MLA_V))
        s = lax.dot_general(qn_ref[:, h * MLA_NOPE:(h + 1) * MLA_NOPE], kv_ref[:, kcols], _NT,
                            preferred_element_type=F32)
        s = s + lax.dot_general(qp_ref[:, h * MLA_ROPE:(h + 1) * MLA_ROPE], kpe, _NT,
                                preferred_element_type=F32)
        o_ref[:, h * MLA_V:(h + 1) * MLA_V] = _softmax_pv(s * scale, kv_ref[:, vcols]).astype(o_ref.dtype)


def mla_attention(q_nope, q_pe, kv, k_pe, tq=256):
    b, s, _ = q_nope.shape
    sk = kv.shape[1]
    hp = MLA_HEADS_PER_STEP
    return pl.pallas_call(
        functools.partial(_mla_kernel, scale=(MLA_NOPE + MLA_ROPE) ** -0.5),
        out_shape=jax.ShapeDtypeStruct((b, s, MLA_HEADS * MLA_V), BF16),
        grid=(b, MLA_HEADS // hp, s // tq),
        in_specs=[
            pl.BlockSpec((None, tq, hp * MLA_NOPE), lambda bi, g, qi: (bi, qi, g)),
            pl.BlockSpec((None, tq, hp * MLA_ROPE), lambda bi, g, qi: (bi, qi, g)),
            pl.BlockSpec((None, sk, hp * (MLA_NOPE + MLA_V)), lambda bi, g, qi: (bi, 0, g)),
            pl.BlockSpec((None, sk, MLA_ROPE), lambda bi, g, qi: (bi, 0, 0)),
        ],
        out_specs=pl.BlockSpec((None, tq, hp * MLA_V), lambda bi, g, qi: (bi, qi, g)),
        compiler_params=_params(("arbitrary", "arbitrary", "arbitrary")),
        name="mla_attention",
    )(q_nope, q_pe, kv, k_pe)


GDN_VREG_ROWS = 8


GDN_SOLVE_BLOCK = 16
GDN_HEADS_PER_STEP = 4


def _solve_unit_triangular(problems):
    n_rows = problems[0][0].shape[0]
    vr = GDN_VREG_ROWS
    n_groups = n_rows // vr
    gpb = GDN_SOLVE_BLOCK // vr
    n_blocks = n_rows // GDN_SOLVE_BLOCK
    all_groups = [[rhs[g * vr:(g + 1) * vr, :] for g in range(n_groups)] for _, rhs, _ in problems]
    for bi in range(n_blocks):
        for t in range(GDN_SOLVE_BLOCK):
            for (a, _, reverse), groups in zip(problems, all_groups):
                b = n_blocks - 1 - bi if reverse else bi
                g_lo, g_hi = b * gpb, (b + 1) * gpb
                j = g_hi * vr - 1 - t if reverse else g_lo * vr + t
                gj, sj = divmod(j, vr)
                xj = groups[gj][sj:sj + 1, :]
                for g in (range(gj, g_lo - 1, -1) if reverse else range(gj, g_hi)):
                    groups[g] = groups[g] - a[g * vr:(g + 1) * vr, j:j + 1] * xj
        if bi == n_blocks - 1:
            break
        for (a, _, reverse), groups in zip(problems, all_groups):
            b = n_blocks - 1 - bi if reverse else bi
            g_lo, g_hi = b * gpb, (b + 1) * gpb
            rest = range(0, g_lo) if reverse else range(g_hi, n_groups)
            zero = jnp.zeros_like(groups[0])
            x_blk = jnp.concatenate([groups[g] if g_lo <= g < g_hi else zero for g in range(n_groups)], axis=0)
            upd = jnp.dot(a[rest[0] * vr:(rest[-1] + 1) * vr, :], x_blk, precision=lax.Precision.HIGHEST,
                          preferred_element_type=F32)
            for t, g in enumerate(rest):
                groups[g] = groups[g] - upd[t * vr:(t + 1) * vr, :]
    return [jnp.concatenate(groups, axis=0) for groups in all_groups]


def _gdn_chunks(chunks):
    L = chunks[0][0].shape[0]
    ii = lax.broadcasted_iota(jnp.int32, (L, L), 0)
    jj = lax.broadcasted_iota(jnp.int32, (L, L), 1)
    pre = []
    for q, k, v, gc, gc_row, gc_end, beta, state, reverse in chunks:
        keep = (ii <= jj) if reverse else (ii >= jj)
        strict = (ii < jj) if reverse else (ii > jj)
        gamma = jnp.where(keep, jnp.exp(jnp.where(keep, gc - gc_row, 0.0)), 0.0)
        kb = k.astype(BF16)
        kk = lax.dot_general(kb, kb, _NT, preferred_element_type=F32)
        qk = lax.dot_general(q.astype(BF16), kb, _NT, preferred_element_type=F32)
        a = jnp.where(strict, kk * gamma * beta, 0.0)
        eg = jnp.exp(gc)
        pre.append((a, jnp.concatenate([v * beta, k * (beta * eg)], axis=1), (qk * gamma).astype(BF16), eg))
    xs = _solve_unit_triangular([(a, rhs, c[8]) for (a, rhs, _, _), c in zip(pre, chunks)])
    out = []
    for x, (_, _, qkg, eg), (q, k, v, gc, gc_row, gc_end, beta, state, reverse) in zip(xs, pre, chunks):
        dv = v.shape[1]
        u, w = x[:, :dv], x[:, dv:]
        ws = jnp.dot(jnp.concatenate([w, q * eg], axis=0).astype(BF16), state.astype(BF16),
                     preferred_element_type=F32)
        v_new = (u - ws[:L]).astype(BF16)
        o = ws[L:] + jnp.dot(qkg, v_new, preferred_element_type=F32)
        k_dec = (k * jnp.exp(gc_end - gc)).astype(BF16)
        new_state = state * jnp.exp(gc_end) + lax.dot_general(k_dec, v_new, (((0,), (0,)), ((), ())),
                                                              preferred_element_type=F32)
        out.append((o, new_state))
    return out


def _gdn_kernel(q_ref, k_ref, v_ref, z_ref, gcol_ref, grow_ref, s0_ref, nw_ref, o_ref, s_out_ref,
                of_scr, ob_scr, st_scr, *, n_chunks):
    L = GDN_CHUNK
    st_scr[...] = s0_ref[...]
    ii = lax.broadcasted_iota(jnp.int32, (L, L), 0)
    jj = lax.broadcasted_iota(jnp.int32, (L, L), 1)
    lower = (ii >= jj).astype(F32)
    upper = (ii <= jj).astype(F32)
    hi = lax.Precision.HIGHEST

    def chunk_step(c, carry):
        chunks, where = [], []
        for hh in range(GDN_HEADS_PER_STEP):
            cols = slice(hh * HEAD_DIM, (hh + 1) * HEAD_DIM)
            for d, reverse in ((0, False), (1, True)):
                cc = (n_chunks - 1 - c) if reverse else c
                r0 = pl.multiple_of(cc * L, L)
                q = q_ref[pl.ds(r0, L), cols]
                k = k_ref[pl.ds(r0, L), cols]
                v = v_ref[pl.ds(r0, L), cols]
                q = q * lax.rsqrt(jnp.sum(q * q, axis=-1, keepdims=True) + NORM_EPS) * (HEAD_DIM ** -0.5)
                k = k * lax.rsqrt(jnp.sum(k * k, axis=-1, keepdims=True) + NORM_EPS)
                gcol = gcol_ref[hh, cc]
                grow = grow_ref[hh, cc]
                csum = jnp.dot(upper if reverse else lower, gcol, precision=hi, preferred_element_type=F32)
                rsum = jnp.dot(grow, lower if reverse else upper, precision=hi, preferred_element_type=F32)
                gc = csum[:, d:d + 1]
                gc_end = gc[0:1, :] if reverse else gc[L - 1:L, :]
                chunks.append((q, k, v, gc, rsum[d:d + 1, :], gc_end, gcol[:, 2 + d:3 + d], st_scr[d, hh], reverse))
                where.append((d, hh, r0, cols))
        for (o, new_state), (d, hh, r0, cols) in zip(_gdn_chunks(chunks), where):
            st_scr[d, hh] = new_state
            (ob_scr if d else of_scr)[pl.ds(r0, L), cols] = o
        return carry

    lax.fori_loop(0, n_chunks, chunk_step, 0)
    s_out_ref[...] = st_scr[...]
    for hh in range(GDN_HEADS_PER_STEP):
        cols = slice(hh * HEAD_DIM, (hh + 1) * HEAD_DIM)
        o = of_scr[:, cols] + ob_scr[:, cols]
        o = o * lax.rsqrt(jnp.mean(o * o, axis=-1, keepdims=True) + NORM_EPS) * nw_ref[...]
        z = z_ref[:, cols]
        o_ref[:, cols] = (o * (z * (1.0 / (1.0 + jnp.exp(-z))))).astype(o_ref.dtype)


def gdn_mixer(qkv, proj, z_col0, g, beta, s0, norm_w, n_batch, seq, row0):
    L = GDN_CHUNK
    nc = seq // L
    gb = jnp.concatenate([g, beta], axis=2)
    gcol = jnp.transpose(gb, (0, 3, 1, 2)).reshape(n_batch, GDN_HEADS, nc, L, 4)
    grow = jnp.transpose(g, (0, 3, 2, 1)).reshape(n_batch, GDN_HEADS, 2, nc, L)
    grow = jnp.pad(jnp.swapaxes(grow, 2, 3), ((0, 0), (0, 0), (0, 0), (0, GDN_VREG_ROWS - 2), (0, 0)))
    rb = row0 // seq
    hps = GDN_HEADS_PER_STEP
    bw = hps * HEAD_DIM
    n_hb = GDN_HEADS // hps
    assert row0 % seq == 0 and z_col0 % bw == 0
    zb = z_col0 // bw
    o, s_out = pl.pallas_call(
        functools.partial(_gdn_kernel, n_chunks=nc),
        out_shape=[jax.ShapeDtypeStruct((n_batch * seq, GDN_WIDTH), BF16),
                   jax.ShapeDtypeStruct((n_batch, 2, GDN_HEADS, HEAD_DIM, HEAD_DIM), F32)],
        grid=(n_batch, n_hb),
        in_specs=[
            pl.BlockSpec((seq, bw), lambda b, h: (rb + b, h)),
            pl.BlockSpec((seq, bw), lambda b, h: (rb + b, n_hb + h)),
            pl.BlockSpec((seq, bw), lambda b, h: (rb + b, 2 * n_hb + h)),
            pl.BlockSpec((seq, bw), lambda b, h: (rb + b, zb + h)),
            pl.BlockSpec((None, hps, nc, L, 4), lambda b, h: (b, h, 0, 0, 0)),
            pl.BlockSpec((None, hps, nc, GDN_VREG_ROWS, L), lambda b, h: (b, h, 0, 0, 0)),
            pl.BlockSpec((None, 2, hps, HEAD_DIM, HEAD_DIM), lambda b, h: (b, 0, h, 0, 0)),
            pl.BlockSpec((1, HEAD_DIM), lambda b, h: (0, 0)),
        ],
        out_specs=[
            pl.BlockSpec((seq, bw), lambda b, h: (b, h)),
            pl.BlockSpec((None, 2, hps, HEAD_DIM, HEAD_DIM), lambda b, h: (b, 0, h, 0, 0)),
        ],
        scratch_shapes=[pltpu.VMEM((seq, bw), F32), pltpu.VMEM((seq, bw), F32),
                        pltpu.VMEM((2, hps, HEAD_DIM, HEAD_DIM), F32)],
        compiler_params=_params(("arbitrary", "arbitrary")),
        name="gdn_mixer",
    )(qkv, qkv, qkv, proj, gcol, grow, s0, norm_w.reshape(1, HEAD_DIM))
    return o, s_out


SSD_HEADS_PER_GROUP = SSD_HEADS // SSD_GROUPS
SSD_GROUP_WIDTH = SSD_HEADS_PER_GROUP * SSD_HEAD_DIM


def _ssd_head_chunk(xdt, cb, bm, cm, cs, cs_row, cs_end, state, reverse):
    L = xdt.shape[0]
    ii = lax.broadcasted_iota(jnp.int32, (L, L), 0)
    jj = lax.broadcasted_iota(jnp.int32, (L, L), 1)
    keep = (ii <= jj) if reverse else (ii >= jj)
    decay = jnp.where(keep, jnp.exp(jnp.where(keep, cs - cs_row, 0.0)), 0.0)
    y = jnp.dot((cb * decay).astype(BF16), xdt, preferred_element_type=F32)
    y = y + lax.dot_general((cm * jnp.exp(cs)).astype(BF16), state.astype(BF16), _NT, preferred_element_type=F32)
    b_dec = (bm * jnp.exp(cs_end - cs)).astype(BF16)
    new_state = state * jnp.exp(cs_end) + lax.dot_general(xdt, b_dec, (((0,), (0,)), ((), ())),
                                                          preferred_element_type=F32)
    return y, new_state


def _ssd_kernel(x_ref, b_ref, c_ref, dtc_ref, dac_ref, dar_ref, h0_ref, dsk_ref, y_ref, h_out_ref,
                yf_scr, yb_scr, st_scr, *, n_chunks):
    L = SSD_CHUNK
    nh = SSD_HEADS_PER_GROUP
    st_scr[...] = h0_ref[...]
    ii = lax.broadcasted_iota(jnp.int32, (L, L), 0)
    jj = lax.broadcasted_iota(jnp.int32, (L, L), 1)
    lower = (ii >= jj).astype(F32)
    upper = (ii <= jj).astype(F32)
    hi = lax.Precision.HIGHEST

    def chunk_step(c, carry):
        for d, reverse in ((0, False), (1, True)):
            cc = (n_chunks - 1 - c) if reverse else c
            r0 = pl.multiple_of(cc * L, L)
            x = x_ref[pl.ds(r0, L), :]
            bm = b_ref[pl.ds(r0, L), :]
            cm = c_ref[pl.ds(r0, L), :]
            cb = lax.dot_general(cm.astype(BF16), bm.astype(BF16), _NT, preferred_element_type=F32)
            dtc = dtc_ref[cc]
            csum = jnp.dot(upper if reverse else lower, dac_ref[cc], precision=hi, preferred_element_type=F32)
            rsum = jnp.dot(dar_ref[cc], lower if reverse else upper, precision=hi, preferred_element_type=F32)
            ys = []
            for j in range(nh):
                col = d * nh + j
                xdt = (x[:, j * SSD_HEAD_DIM:(j + 1) * SSD_HEAD_DIM] * dtc[:, col:col + 1]).astype(BF16)
                cs = csum[:, col:col + 1]
                cs_end = cs[0:1, :] if reverse else cs[L - 1:L, :]
                y, new_state = _ssd_head_chunk(xdt, cb, bm, cm, cs, rsum[col:col + 1, :], cs_end, st_scr[d, j],
                                               reverse)
                st_scr[d, j] = new_state
                ys.append(y)
            (yb_scr if reverse else yf_scr)[pl.ds(r0, L), :] = jnp.concatenate(ys, axis=1)
        return carry

    lax.fori_loop(0, n_chunks, chunk_step, 0)
    h_out_ref[...] = st_scr[...]
    y_ref[...] = yf_scr[...] + yb_scr[...] + dsk_ref[...] * x_ref[...]


def ssd_mixer(xbc, dt, a, d_skip, h0, n_batch, seq, row0):
    L = SSD_CHUNK
    nc = seq // L
    nh = SSD_HEADS_PER_GROUP

    def group_cols(t):
        t = t.reshape(n_batch, seq, 2, SSD_GROUPS, nh)
        return jnp.transpose(t, (0, 3, 1, 2, 4)).reshape(n_batch, SSD_GROUPS, nc, L, 2 * nh)

    dtc = group_cols(dt)
    dac = group_cols(dt * a)
    dar = jnp.swapaxes(dac, 3, 4)
    dsk = jnp.repeat(jnp.sum(d_skip, axis=0), SSD_HEAD_DIM).reshape(1, SSD_INNER)
    rb = row0 // seq
    assert row0 % seq == 0
    gw = SSD_GROUP_WIDTH
    b_blk0 = SSD_INNER // SSD_STATE
    c_blk0 = b_blk0 + SSD_GROUPS
    y, h_out = pl.pallas_call(
        functools.partial(_ssd_kernel, n_chunks=nc),
        out_shape=[jax.ShapeDtypeStruct((n_batch * seq, SSD_INNER), F32),
                   jax.ShapeDtypeStruct((n_batch, 2, SSD_HEADS, SSD_HEAD_DIM, SSD_STATE), F32)],
        grid=(n_batch, SSD_GROUPS),
        in_specs=[
            pl.BlockSpec((seq, gw), lambda b, g: (rb + b, g)),
            pl.BlockSpec((seq, SSD_STATE), lambda b, g: (rb + b, b_blk0 + g)),
            pl.BlockSpec((seq, SSD_STATE), lambda b, g: (rb + b, c_blk0 + g)),
            pl.BlockSpec((None, None, nc, L, 2 * nh), lambda b, g: (b, g, 0, 0, 0)),
            pl.BlockSpec((None, None, nc, L, 2 * nh), lambda b, g: (b, g, 0, 0, 0)),
            pl.BlockSpec((None, None, nc, 2 * nh, L), lambda b, g: (b, g, 0, 0, 0)),
            pl.BlockSpec((None, 2, nh, SSD_HEAD_DIM, SSD_STATE), lambda b, g: (b, 0, g, 0, 0)),
            pl.BlockSpec((1, gw), lambda b, g: (0, g)),
        ],
        out_specs=[
            pl.BlockSpec((seq, gw), lambda b, g: (b, g)),
            pl.BlockSpec((None, 2, nh, SSD_HEAD_DIM, SSD_STATE), lambda b, g: (b, 0, g, 0, 0)),
        ],
        scratch_shapes=[pltpu.VMEM((seq, gw), F32), pltpu.VMEM((seq, gw), F32),
                        pltpu.VMEM((2, nh, SSD_HEAD_DIM, SSD_STATE), F32)],
        compiler_params=_params(("arbitrary", "arbitrary")),
        name="ssd_mixer",
    )(xbc, xbc, xbc, dtc, dac, dar, h0, dsk)
    return y, h_out


def _gated_norm_kernel(y_ref, z0_ref, z1_ref, w_ref, o_ref):
    z = jnp.concatenate([z0_ref[...], z1_ref[...]], axis=1)
    t = y_ref[...] * (z * (1.0 / (1.0 + jnp.exp(-z))))
    t = t * lax.rsqrt(jnp.mean(t * t, axis=-1, keepdims=True) + NORM_EPS) * w_ref[...]
    o_ref[...] = t.astype(o_ref.dtype)


def ssd_gated_norm(y, proj, z_col0, norm_w):
    n, d = y.shape
    tm = ROW_TILE
    half = d // 2
    assert z_col0 % half == 0
    zb = z_col0 // half
    return pl.pallas_call(
        _gated_norm_kernel,
        out_shape=jax.ShapeDtypeStruct((n, d), BF16),
        grid=(n // tm,),
        in_specs=[pl.BlockSpec((tm, d), lambda i: (i, 0)),
                  pl.BlockSpec((tm, half), lambda i: (i, zb)),
                  pl.BlockSpec((tm, half), lambda i: (i, zb + 1)),
                  pl.BlockSpec((1, d), lambda i: (0, 0))],
        out_specs=pl.BlockSpec((tm, d), lambda i: (i, 0)),
        compiler_params=_params(("arbitrary",)),
        name="ssd_gated_norm",
    )(y, proj, proj, norm_w.reshape(1, d))


ITEM_VALID = 1
ITEM_NEW_WEIGHTS = 2


def _moe_up_kernel(ib_ref, ic_ref, iwe_ref, iwc_ref, iflag_ref, x_ref, wg_ref, wu_ref, bg_ref, bu_ref, o_ref,
                   wg_bf, wu_bf):
    flag = iflag_ref[pl.program_id(0)]

    @pl.when((flag & ITEM_NEW_WEIGHTS) != 0)
    def _():
        wg_bf[...] = wg_ref[...].astype(BF16)
        wu_bf[...] = wu_ref[...].astype(BF16)

    @pl.when((flag & ITEM_VALID) != 0)
    def _():
        x = x_ref[...]
        gate = jnp.minimum(jnp.dot(x, wg_bf[...], preferred_element_type=F32) + bg_ref[...], SWIGLU_LIMIT)
        up = jnp.clip(jnp.dot(x, wu_bf[...], preferred_element_type=F32) + bu_ref[...], -SWIGLU_LIMIT, SWIGLU_LIMIT)
        act = (up + 1.0) * gate * (1.0 / (1.0 + jnp.exp(-SWIGLU_ALPHA * gate)))
        o_ref[...] = act.astype(o_ref.dtype)

    @pl.when((flag & ITEM_VALID) == 0)
    def _():
        o_ref[...] = jnp.zeros_like(o_ref)


def _moe_down_kernel(ib_ref, ic_ref, iwe_ref, iwc_ref, iflag_ref, a_ref, wd_ref, bd_ref, o_ref, wd_bf):
    flag = iflag_ref[pl.program_id(0)]

    @pl.when((flag & ITEM_NEW_WEIGHTS) != 0)
    def _():
        wd_bf[...] = wd_ref[...].astype(BF16)

    @pl.when((flag & ITEM_VALID) != 0)
    def _():
        o_ref[...] = jnp.dot(a_ref[...], wd_bf[...], preferred_element_type=F32) + bd_ref[...]

    @pl.when((flag & ITEM_VALID) == 0)
    def _():
        o_ref[...] = jnp.zeros_like(o_ref)


def _moe_items(blk_start, nblk, n_used, n_chunks):
    n_items = MOE_NBLK * n_chunks
    i = jnp.arange(n_items, dtype=jnp.int32)
    item_end = (blk_start + nblk) * n_chunks
    e = jnp.minimum(jnp.searchsorted(item_end, i, side='right'), N_EXPERTS - 1).astype(jnp.int32)
    r = i - blk_start[e] * n_chunks
    nb = jnp.maximum(nblk[e], 1)
    valid = i < n_used * n_chunks
    n_tail = jnp.maximum(MOE_NBLK - n_used, 1)
    rt = i - n_used * n_chunks
    chunk = jnp.where(valid, r // nb, rt // n_tail).astype(jnp.int32)
    blk = jnp.where(valid, blk_start[e] + r % nb, n_used + rt % n_tail).astype(jnp.int32)
    last = jnp.maximum(n_used * n_chunks - 1, 0)
    we = jnp.where(valid, e, e[last]).astype(jnp.int32)
    wc = jnp.where(valid, chunk, chunk[last]).astype(jnp.int32)
    new_w = jnp.concatenate([jnp.ones((1,), bool), (we[1:] != we[:-1]) | (wc[1:] != wc[:-1])])
    flag = valid.astype(jnp.int32) * ITEM_VALID + new_w.astype(jnp.int32) * ITEM_NEW_WEIGHTS
    return blk, chunk, we, wc, flag


MOE_COMBINE_TT = 64


def _combine_row_copy(y_hbm, buf, sem, slot, src_row, k, t):
    return pltpu.make_async_copy(y_hbm.at[pl.ds(src_row, 1), :], buf.at[slot, k, pl.ds(t, 1), :], sem.at[slot])


def _moe_combine_kernel(pos_ref, pos_next_ref, y_hbm, rw_ref, x_ref, gate_ref, o_ref, buf, sem):
    i = pl.program_id(0)
    n = pl.num_programs(0)
    tt = MOE_COMBINE_TT

    def start_all(p_ref, slot):
        def body(r, carry):
            _combine_row_copy(y_hbm, buf, sem, slot, p_ref[0, r], r % TOP_K, r // TOP_K).start()
            return carry
        lax.fori_loop(0, tt * TOP_K, body, 0)

    @pl.when(i == 0)
    def _():
        start_all(pos_ref, 0)

    @pl.when(i + 1 < n)
    def _():
        start_all(pos_next_ref, (i + 1) % 2)

    slot = i % 2

    def wait_body(r, carry):
        _combine_row_copy(y_hbm, buf, sem, slot, 0, r % TOP_K, r // TOP_K).wait()
        return carry
    lax.fori_loop(0, tt * TOP_K, wait_body, 0)

    rw = rw_ref[...]
    ffn = buf[slot, 0] * rw[:, 0:1]
    for k in range(1, TOP_K):
        ffn = ffn + buf[slot, k] * rw[:, k:k + 1]
    o_ref[...] = x_ref[...] + gate_ref[...] * ffn


def moe_combine(y, slot_of_assign, router_weights, x, gate):
    t, d = x.shape
    tt = MOE_COMBINE_TT
    n_steps = t // tt
    pos = slot_of_assign.reshape(n_steps, 1, tt * TOP_K)
    smem_spec = functools.partial(pl.BlockSpec, (None, 1, tt * TOP_K), memory_space=pltpu.SMEM)
    return pl.pallas_call(
        _moe_combine_kernel,
        out_shape=jax.ShapeDtypeStruct((t, d), F32),
        grid=(n_steps,),
        in_specs=[
            smem_spec(lambda i: (i, 0, 0)),
            smem_spec(lambda i: (jnp.minimum(i + 1, n_steps - 1), 0, 0)),
            pl.BlockSpec(memory_space=pl.ANY),
            pl.BlockSpec((tt, TOP_K), lambda i: (i, 0)),
            pl.BlockSpec((tt, d), lambda i: (i, 0)),
            pl.BlockSpec((None, 1, d), lambda i: (_cond_row_of_tile(i, tt), 0, 0)),
        ],
        out_specs=pl.BlockSpec((tt, d), lambda i: (i, 0)),
        scratch_shapes=[pltpu.VMEM((2, TOP_K, tt, d), F32), pltpu.SemaphoreType.DMA((2,))],
        compiler_params=_params(("arbitrary",)),
        name="moe_combine",
    )(pos, pos, y, router_weights, x, gate)


def moe_ffn(h, logits, layer, w_gate, b_gate, w_up, b_up, w_down, b_down, x, gate):
    t, d = h.shape
    n_assign = t * TOP_K
    top_logit, top_idx = lax.top_k(logits, TOP_K)
    gates = jax.nn.softmax(top_logit, axis=-1)
    flat_e = top_idx.reshape(n_assign).astype(jnp.int32)
    order = jnp.argsort(flat_e).astype(jnp.int32)
    sorted_e = flat_e[order]
    counts = jnp.bincount(flat_e, length=N_EXPERTS).astype(jnp.int32)
    nblk = (counts + MOE_TM - 1) // MOE_TM
    blk_end = jnp.cumsum(nblk).astype(jnp.int32)
    blk_start = blk_end - nblk
    n_used = blk_end[-1]
    sort_start = jnp.cumsum(counts).astype(jnp.int32) - counts
    dest = blk_start[sorted_e] * MOE_TM + jnp.arange(n_assign, dtype=jnp.int32) - sort_start[sorted_e]
    slot_tok = jnp.full((MOE_SLOTS,), t, jnp.int32).at[dest].set(order // TOP_K)
    slot_of_assign = jnp.zeros((n_assign,), jnp.int32).at[order].set(dest).reshape(t, TOP_K)
    x_slots = jnp.concatenate([h, jnp.zeros((1, d), h.dtype)], axis=0)[slot_tok]

    depth = w_gate.shape[0]
    n_fc = EXPERT_FF // MOE_TF
    items = _moe_items(blk_start, nblk, n_used, n_fc)
    act = pl.pallas_call(
        _moe_up_kernel,
        out_shape=jax.ShapeDtypeStruct((MOE_SLOTS, EXPERT_FF), BF16),
        grid_spec=pltpu.PrefetchScalarGridSpec(
            num_scalar_prefetch=5,
            grid=(MOE_NBLK * n_fc,),
            in_specs=[
                pl.BlockSpec((MOE_TM, d), lambda i, ib, ic, iwe, iwc, ifl: (ib[i], 0)),
                pl.BlockSpec((None, None, d, MOE_TF), lambda i, ib, ic, iwe, iwc, ifl: (layer, iwe[i], 0, iwc[i])),
                pl.BlockSpec((None, None, d, MOE_TF), lambda i, ib, ic, iwe, iwc, ifl: (layer, iwe[i], 0, iwc[i])),
                pl.BlockSpec((None, None, 1, MOE_TF), lambda i, ib, ic, iwe, iwc, ifl: (layer, iwe[i], 0, iwc[i])),
                pl.BlockSpec((None, None, 1, MOE_TF), lambda i, ib, ic, iwe, iwc, ifl: (layer, iwe[i], 0, iwc[i])),
            ],
            out_specs=pl.BlockSpec((MOE_TM, MOE_TF), lambda i, ib, ic, iwe, iwc, ifl: (ib[i], ic[i])),
            scratch_shapes=[pltpu.VMEM((d, MOE_TF), BF16), pltpu.VMEM((d, MOE_TF), BF16)],
        ),
        compiler_params=_params(("arbitrary",)),
        name="moe_gate_up",
    )(*items, x_slots, w_gate, w_up,
      b_gate.reshape(depth, N_EXPERTS, 1, EXPERT_FF), b_up.reshape(depth, N_EXPERTS, 1, EXPERT_FF))

    n_nc = d // MOE_TN
    items = _moe_items(blk_start, nblk, n_used, n_nc)
    y = pl.pallas_call(
        _moe_down_kernel,
        out_shape=jax.ShapeDtypeStruct((MOE_SLOTS, d), F32),
        grid_spec=pltpu.PrefetchScalarGridSpec(
            num_scalar_prefetch=5,
            grid=(MOE_NBLK * n_nc,),
            in_specs=[
                pl.BlockSpec((MOE_TM, EXPERT_FF), lambda i, ib, ic, iwe, iwc, ifl: (ib[i], 0)),
                pl.BlockSpec((None, None, EXPERT_FF, MOE_TN),
                             lambda i, ib, ic, iwe, iwc, ifl: (layer, iwe[i], 0, iwc[i])),
                pl.BlockSpec((None, None, 1, MOE_TN), lambda i, ib, ic, iwe, iwc, ifl: (layer, iwe[i], 0, iwc[i])),
            ],
            out_specs=pl.BlockSpec((MOE_TM, MOE_TN), lambda i, ib, ic, iwe, iwc, ifl: (ib[i], ic[i])),
            scratch_shapes=[pltpu.VMEM((EXPERT_FF, MOE_TN), BF16)],
        ),
        compiler_params=_params(("arbitrary",)),
        name="moe_down",
    )(*items, act, w_down, b_down.reshape(depth, N_EXPERTS, 1, d))
    return moe_combine(y, slot_of_assign, gates, x, gate)


def _split(x, sizes):
    return jnp.split(x, np.cumsum(sizes)[:-1].tolist(), axis=-1)


def _rms(x, w):
    return (x * lax.rsqrt(jnp.mean(x * x, axis=-1, keepdims=True) + NORM_EPS)) * w


def _rope_1d(x, pos):
    half = x.shape[-1] // 2
    inv_freq = ROPE_BASE ** (-jnp.arange(half, dtype=F32) / half)
    ang = pos.astype(F32)[:, None] * inv_freq
    cos = jnp.cos(ang)[:, None, :]
    sin = jnp.sin(ang)[:, None, :]
    x1, x2 = x[..., :half], x[..., half:]
    return jnp.concatenate([x1 * cos - x2 * sin, x2 * cos + x1 * sin], axis=-1)


def _rope_2d(x, n_rows):
    t = jnp.arange(n_rows * GRID_W)
    d_axis = x.shape[-1] // 2
    return jnp.concatenate([_rope_1d(x[..., :d_axis], t // GRID_W), _rope_1d(x[..., d_axis:], t % GRID_W)], axis=-1)


def _short_conv(x, w):
    pad = (w.shape[0] - 1) // 2
    s = x.shape[1]
    xp = jnp.pad(x, ((0, 0), (pad, pad), (0, 0)))
    out = xp[:, 0:s] * w[0]
    for j in range(1, w.shape[0]):
        out = out + xp[:, j:j + s] * w[j]
    return out


def _conv_silu_merged(cols, conv_w, conv_b=None):
    outs = []
    for t in (cols[:N_PROMPT].reshape(BATCH, SEQ, -1), cols[N_PROMPT:].reshape(DEC_BATCH, DEC_SEQ, -1)):
        t = _short_conv(t, conv_w)
        if conv_b is not None:
            t = t + conv_b
        outs.append(jax.nn.silu(t).reshape(-1, t.shape[-1]))
    return jnp.concatenate(outs, axis=0)


def _even_attention_pass(proj, q_norm, k_norm, ctx, n_rows):
    b, s, _ = proj.shape
    q, k, v = _split(proj[..., :sum(EVEN_SPLITS[:3])], EVEN_SPLITS[:3])
    q = _rms(q.reshape(b, s, GQA_HEADS, HEAD_DIM), q_norm)
    k = _rms(k.reshape(b, s, GQA_KV_HEADS, HEAD_DIM), k_norm)
    v = v.reshape(b, s, GQA_KV_HEADS, HEAD_DIM)
    if ctx is None:
        k_all, v_all = k, v
    else:
        k_ctx, v_ctx = ctx
        q = _rope_2d(q, n_rows)
        k_all = jnp.concatenate([k_ctx, _rope_2d(k, n_rows)], axis=1)
        v_all = jnp.concatenate([v_ctx, v], axis=1)
    sk = k_all.shape[1]
    attn = gqa_attention(q.reshape(b, s, -1).astype(BF16), k_all.reshape(b, sk, -1).astype(BF16),
                         v_all.reshape(b, sk, -1).astype(BF16))
    return attn, k, v


def kernel(x_prompt, x_sample, cache_gqa_k, cache_gqa_v, state_ssd, state_gdn, cache_mla_ckv, cache_mla_kpe, c, c_ctx, w_mod, b_mod, norm_mix, norm_ffn, ev_w_in, ev_w_out, gqa_q_norm, gqa_k_norm, ssd_conv_w, ssd_conv_b, ssd_a_log, ssd_dt_bias, ssd_d_skip, ssd_norm, od_w_in, od_w_out, gdn_conv_w, gdn_a_log, gdn_dt_bias, gdn_norm, mla_q_a_norm, mla_w_q_b, mla_kv_a_norm, mla_w_kv_b, router_w, router_b, moe_w_gate, moe_b_gate, moe_w_up, moe_b_up, moe_w_down, moe_b_down, final_norm):
    n_rows = DEC_SEQ // GRID_W
    x = jnp.concatenate([x_prompt.reshape(N_PROMPT, D_MODEL), x_sample.reshape(N_SAMPLE, D_MODEL)], axis=0)

    cond = jnp.concatenate([c_ctx[None, :], c, jnp.zeros((N_COND_PAD - N_COND, D_MODEL), F32)], axis=0)
    mods = modulation(jax.nn.silu(cond).astype(BF16), w_mod, b_mod)
    mods = mods.reshape(DEPTH, N_COND_PAD, N_MOD, 1, D_MODEL)

    def split_passes(t):
        return (t[:N_PROMPT].reshape(BATCH, SEQ, -1), t[N_PROMPT:].reshape(DEC_BATCH, DEC_SEQ, -1))

    def merge_passes(tp, ts):
        return jnp.concatenate([tp.reshape(N_PROMPT, -1), ts.reshape(N_SAMPLE, -1)], axis=0)

    ctx_even, ctx_odd = [], []
    for l in range(DEPTH):
        i = l // 2
        sh1, sc1, g1, sh2, sc2, g2 = (mods[l, :, j] for j in range(N_MOD))
        h = norm_modulate(x, norm_mix, l, sc1, sh1)
        if l % 2 == 0:
            proj = matmul(h, ev_w_in, (i,), n_cols=EVEN_MAIN, name="even_in_proj")
            dt_proj = matmul(h, ev_w_in[i, :, EVEN_MAIN:], name="even_dt_proj")
            pp, ps = split_passes(proj)
            attn_p, k_p, v_p = _even_attention_pass(pp, gqa_q_norm[i], gqa_k_norm[i], None, None)
            attn_s, _, _ = _even_attention_pass(ps, gqa_q_norm[i], gqa_k_norm[i],
                                                (cache_gqa_k[:, i], cache_gqa_v[:, i]), n_rows)
            xbc = _conv_silu_merged(proj[:, EVEN_MAIN - SSD_CONV_CH:], ssd_conv_w[i], ssd_conv_b[i])
            dt = jax.nn.softplus(dt_proj.reshape(N_TOK, 2, SSD_HEADS) + ssd_dt_bias[i])
            dt_p, dt_s = dt[:N_PROMPT].reshape(BATCH, SEQ, 2, SSD_HEADS), dt[N_PROMPT:].reshape(
                DEC_BATCH, DEC_SEQ, 2, SSD_HEADS)
            a = -jnp.exp(ssd_a_log[i])
            y_p, h_p = ssd_mixer(xbc, dt_p, a, ssd_d_skip[i],
                                 jnp.zeros((BATCH, 2, SSD_HEADS, SSD_HEAD_DIM, SSD_STATE), F32), BATCH, SEQ, 0)
            y_s, _ = ssd_mixer(xbc, dt_s, a, ssd_d_skip[i], state_ssd[:, i], DEC_BATCH, DEC_SEQ, N_PROMPT)
            yn = ssd_gated_norm(jnp.concatenate([y_p, y_s], axis=0), proj, sum(EVEN_SPLITS[:3]), ssd_norm[i])
            ctx_even.append((k_p, v_p, h_p))
            feat = jnp.concatenate([merge_passes(attn_p, attn_s), yn], axis=-1)
            x = matmul(feat, ev_w_out, (i,), residual=x, gate=g1, name="even_out_proj")
        else:
            proj = matmul(h, od_w_in, (i,), tn=896, name="odd_in_proj")
            off = np.cumsum((0,) + ODD_SPLITS)
            qkv = _conv_silu_merged(proj[:, :off[1]], gdn_conv_w[i])
            gates = proj[:, off[2]:off[3]].reshape(N_TOK, 2, 2, GDN_HEADS)
            g = -jnp.exp(gdn_a_log[i]) * jax.nn.softplus(gates[:, 0] + gdn_dt_bias[i])
            beta = jax.nn.sigmoid(gates[:, 1])

            def gate_passes(t):
                return (t[:N_PROMPT].reshape(BATCH, SEQ, 2, GDN_HEADS),
                        t[N_PROMPT:].reshape(DEC_BATCH, DEC_SEQ, 2, GDN_HEADS))

            (g_p, g_s), (beta_p, beta_s) = gate_passes(g), gate_passes(beta)
            o_p, s_p = gdn_mixer(qkv, proj, int(off[1]), g_p, beta_p,
                                 jnp.zeros((BATCH, 2, GDN_HEADS, HEAD_DIM, HEAD_DIM), F32), gdn_norm[i], BATCH, SEQ, 0)
            o_s, _ = gdn_mixer(qkv, proj, int(off[1]), g_s, beta_s, state_gdn[:, i], gdn_norm[i], DEC_BATCH, DEC_SEQ,
                               N_PROMPT)
            o_p = o_p.reshape(BATCH, SEQ, -1)
            o_s = o_s.reshape(DEC_BATCH, DEC_SEQ, -1)
            qa = _rms(proj[:, off[3]:off[4]], mla_q_a_norm[i]).astype(BF16)
            ckv = _rms(proj[:, off[4]:off[5]], mla_kv_a_norm[i])
            ckv_p, ckv_s = split_passes(ckv)
            kpe_p, kpe_s = split_passes(proj[:, off[5]:off[6]])
            ctx_odd.append((s_p, ckv_p, kpe_p))
            qm = matmul(qa, mla_w_q_b, (i,), name="mla_q_b")
            ckv_all_s = jnp.concatenate([cache_mla_ckv[:, i], ckv_s], axis=1)
            sk_s = PAST_LEN + DEC_SEQ
            kv = matmul(jnp.concatenate([ckv_p.reshape(N_PROMPT, -1), ckv_all_s.reshape(DEC_BATCH * sk_s, -1)],
                                        axis=0).astype(BF16),
                        mla_w_kv_b, (i,), out_dtype=BF16, name="mla_kv_b")
            kv_p = kv[:N_PROMPT].reshape(BATCH, SEQ, -1)
            kv_s = kv[N_PROMPT:].reshape(DEC_BATCH, sk_s, -1)
            qm_p, qm_s = split_passes(qm)
            qm_p = qm_p.reshape(BATCH, SEQ, MLA_HEADS, MLA_NOPE + MLA_ROPE)
            qm_s = qm_s.reshape(DEC_BATCH, DEC_SEQ, MLA_HEADS, MLA_NOPE + MLA_ROPE)
            qpe_s = _rope_2d(qm_s[..., MLA_NOPE:], n_rows)
            kpe_all_s = jnp.concatenate([cache_mla_kpe[:, i], _rope_2d(kpe_s[:, :, None, :], n_rows)[:, :, 0]],
                                        axis=1)
            om_p = mla_attention(qm_p[..., :MLA_NOPE].reshape(BATCH, SEQ, -1).astype(BF16),
                                 qm_p[..., MLA_NOPE:].reshape(BATCH, SEQ, -1).astype(BF16), kv_p,
                                 kpe_p.astype(BF16))
            om_s = mla_attention(qm_s[..., :MLA_NOPE].reshape(DEC_BATCH, DEC_SEQ, -1).astype(BF16),
                                 qpe_s.reshape(DEC_BATCH, DEC_SEQ, -1).astype(BF16), kv_s, kpe_all_s.astype(BF16))
            feat = merge_passes(jnp.concatenate([o_p, om_p], axis=-1), jnp.concatenate([o_s, om_s], axis=-1))
            x = matmul(feat, od_w_out, (i,), residual=x, gate=g1, name="odd_out_proj")
        h, logits = norm_modulate(x, norm_ffn, l, sc2, sh2, router_w, router_b)
        x = moe_ffn(h, logits, l, moe_w_gate, moe_b_gate, moe_w_up, moe_b_up, moe_w_down, moe_b_down, x, g2)

    y = norm_modulate(x, final_norm[None, :], 0, out_dtype=F32)
    y_prompt = y[:N_PROMPT].reshape(BATCH, SEQ, D_MODEL)
    y_sample = y[N_PROMPT:].reshape(DEC_BATCH, DEC_SEQ, D_MODEL)
    new_gqa_k = jnp.stack([t[0] for t in ctx_even], axis=1)
    new_gqa_v = jnp.stack([t[1] for t in ctx_even], axis=1)
    new_ssd = jnp.stack([t[2] for t in ctx_even], axis=1)
    new_gdn = jnp.stack([t[0] for t in ctx_odd], axis=1)
    new_mla_ckv = jnp.stack([t[1] for t in ctx_odd], axis=1)
    new_mla_kpe = jnp.stack([t[2] for t in ctx_odd], axis=1)
    return (y_prompt, y_sample, new_gqa_k, new_gqa_v, new_ssd, new_gdn, new_mla_ckv, new_mla_kpe)
```

```python
import functools
import math

import jax
import jax.numpy as jnp
import numpy as np
from jax import lax
from jax.experimental import pallas as pl
from jax.experimental.pallas import tpu as pltpu

F32 = jnp.float32
BF16 = jnp.bfloat16

D_MODEL = 4096
BATCH = 16
SEQ = 256
DEPTH = 4
DEC_BATCH = 8
DEC_SEQ = 1024
PAST_LEN = 256
GRID_W = 64
HEAD_DIM = 128
ROPE_BASE = 10000.0
NORM_EPS = 1e-6
CONV_WIDTH = 5
GROUP_WIDTH = D_MODEL // 2
N_MOD = 6
GQA_HEADS = GROUP_WIDTH // HEAD_DIM
GQA_KV_HEADS = GQA_HEADS // 4
GQA_GROUP = GQA_HEADS // GQA_KV_HEADS
SSD_INNER = GROUP_WIDTH
SSD_HEAD_DIM = 64
SSD_HEADS = SSD_INNER // SSD_HEAD_DIM
SSD_GROUPS = 4
SSD_STATE = 128
SSD_CHUNK = 128
SSD_CONV_CH = SSD_INNER + 2 * SSD_GROUPS * SSD_STATE
GDN_HEADS = GROUP_WIDTH // HEAD_DIM
GDN_WIDTH = GDN_HEADS * HEAD_DIM
GDN_CHUNK = 64
MLA_HEADS = GROUP_WIDTH // HEAD_DIM
MLA_Q_LORA = D_MODEL // 4
MLA_KV_LORA = 512
MLA_NOPE = 128
MLA_ROPE = 64
MLA_V = 128
N_EXPERTS = 32
TOP_K = 4
EXPERT_FF = 1024
SWIGLU_LIMIT = 7.0
SWIGLU_ALPHA = 1.702
EVEN_SPLITS = (GQA_HEADS * HEAD_DIM, GQA_KV_HEADS * HEAD_DIM, GQA_KV_HEADS * HEAD_DIM, SSD_INNER, SSD_CONV_CH,
               2 * SSD_HEADS)
ODD_SPLITS = (3 * GDN_WIDTH, GDN_WIDTH, 4 * GDN_HEADS, MLA_Q_LORA, MLA_KV_LORA, MLA_ROPE)
EVEN_MAIN = sum(EVEN_SPLITS[:-1])

N_PROMPT = BATCH * SEQ
N_SAMPLE = DEC_BATCH * DEC_SEQ
N_TOK = N_PROMPT + N_SAMPLE
N_COND = 1 + DEC_BATCH
N_COND_PAD = 16

V7X_VMEM_LIMIT_BYTES = 56 * 1024 * 1024

ROW_TILE = 256
MOE_TM = 256
MOE_TF = 512
MOE_TN = 2048
MOE_NBLK = N_TOK * TOP_K // MOE_TM + N_EXPERTS
MOE_SLOTS = MOE_NBLK * MOE_TM


def _cond_row_of_tile(tile, rows_per_tile):
    row0 = tile * rows_per_tile
    return jnp.where(row0 < N_PROMPT, 0, 1 + (row0 - N_PROMPT) // DEC_SEQ)


def _params(semantics):
    return pltpu.CompilerParams(dimension_semantics=semantics, vmem_limit_bytes=V7X_VMEM_LIMIT_BYTES)


def _mm_kernel(*refs, has_bias, has_res):
    a_ref, w_ref = refs[0], refs[1]
    pos = 2
    bias_ref = res_ref = gate_ref = None
    if has_bias:
        bias_ref = refs[pos]
        pos += 1
    if has_res:
        res_ref, gate_ref = refs[pos], refs[pos + 1]
        pos += 2
    o_ref, wbf_ref = refs[pos], refs[pos + 1]

    @pl.when(pl.program_id(1) == 0)
    def _():
        wbf_ref[...] = w_ref[...].astype(BF16)

    acc = jnp.dot(a_ref[...], wbf_ref[...], preferred_element_type=F32)
    if has_bias:
        acc = acc + bias_ref[...]
    if has_res:
        acc = res_ref[...] + gate_ref[...] * acc
    o_ref[...] = acc.astype(o_ref.dtype)


def matmul(a, w, prefix=(), *, n_cols=None, bias=None, residual=None, gate=None, out_dtype=F32, tm=512, tn=512,
           name="mm"):
    m, k = a.shape
    n = w.shape[-1] if n_cols is None else n_cols
    assert w.shape[-2] == k and len(w.shape) == len(prefix) + 2
    tm = min(tm, m)
    tn = min(tn, n)
    assert m % tm == 0 and n % tn == 0
    grid = (n // tn, m // tm)
    npre = len(prefix)
    in_specs = [
        pl.BlockSpec((tm, k), lambda j, i: (i, 0)),
        pl.BlockSpec((None,) * npre + (k, tn), lambda j, i: prefix + (0, j)),
    ]
    args = [a, w]
    if bias is not None:
        b2 = bias.reshape(bias.shape[:-1] + (1, bias.shape[-1]))
        in_specs.append(pl.BlockSpec((None,) * npre + (1, tn), lambda j, i: prefix + (0, j)))
        args.append(b2)
    if residual is not None:
        in_specs.append(pl.BlockSpec((tm, tn), lambda j, i: (i, j)))
        in_specs.append(pl.BlockSpec((None, 1, tn), lambda j, i: (_cond_row_of_tile(i, tm), 0, j)))
        args += [residual, gate]
    return pl.pallas_call(
        functools.partial(_mm_kernel, has_bias=bias is not None, has_res=residual is not None),
        out_shape=jax.ShapeDtypeStruct((m, n), out_dtype),
        grid=grid,
        in_specs=in_specs,
        out_specs=pl.BlockSpec((tm, tn), lambda j, i: (i, j)),
        scratch_shapes=[pltpu.VMEM((k, tn), BF16)],
        compiler_params=_params(("arbitrary", "arbitrary")),
        name=name,
    )(*args)


def _mod_kernel(a_ref, w_ref, b_ref, o_ref):
    acc = jnp.dot(a_ref[...], w_ref[...].astype(BF16), preferred_element_type=F32)
    o_ref[...] = acc + b_ref[...]


def modulation(cond_act, w_mod, b_mod, tn=512):
    depth, k, n = w_mod.shape
    return pl.pallas_call(
        _mod_kernel,
        out_shape=jax.ShapeDtypeStruct((depth, N_COND_PAD, n), F32),
        grid=(depth, n // tn),
        in_specs=[
            pl.BlockSpec((N_COND_PAD, k), lambda l, j: (0, 0)),
            pl.BlockSpec((None, k, tn), lambda l, j: (l, 0, j)),
            pl.BlockSpec((None, 1, tn), lambda l, j: (l, 0, j)),
        ],
        out_specs=pl.BlockSpec((None, N_COND_PAD, tn), lambda l, j: (l, 0, j)),
        compiler_params=_params(("arbitrary", "arbitrary")),
        name="adaln_modulation",
    )(cond_act, w_mod, b_mod.reshape(depth, 1, n))


def _norm_kernel(*refs, modulate, router):
    x_ref, nw_ref = refs[0], refs[1]
    pos = 2
    if modulate:
        sc_ref, sh_ref = refs[pos], refs[pos + 1]
        pos += 2
    if router:
        rw_ref, rb_ref = refs[pos], refs[pos + 1]
        pos += 2
    h_ref = refs[pos]
    x = x_ref[...]
    xn = x * lax.rsqrt(jnp.mean(x * x, axis=-1, keepdims=True) + NORM_EPS)
    h = xn * nw_ref[...]
    if modulate:
        h = h * (1.0 + sc_ref[...]) + sh_ref[...]
    h_ref[...] = h.astype(h_ref.dtype)
    if router:
        logit_ref = refs[pos + 1]
        logit_ref[...] = jnp.dot(h.astype(BF16), rw_ref[...].astype(BF16), preferred_element_type=F32) + rb_ref[...]


def norm_modulate(x, norm_w, layer, scale=None, shift=None, router_w=None, router_b=None, out_dtype=BF16):
    n, d = x.shape
    tm = ROW_TILE
    modulate = scale is not None
    router = router_w is not None
    in_specs = [pl.BlockSpec((tm, d), lambda i: (i, 0)),
                pl.BlockSpec((None, 1, d), lambda i: (layer, 0, 0))]
    args = [x, norm_w.reshape(norm_w.shape[0], 1, d)]
    if modulate:
        spec = pl.BlockSpec((None, 1, d), lambda i: (_cond_row_of_tile(i, tm), 0, 0))
        in_specs += [spec, spec]
        args += [scale, shift]
    out_shape = [jax.ShapeDtypeStruct((n, d), out_dtype)]
    out_specs = [pl.BlockSpec((tm, d), lambda i: (i, 0))]
    if router:
        in_specs += [pl.BlockSpec((None, d, N_EXPERTS), lambda i: (layer, 0, 0)),
                     pl.BlockSpec((None, 1, N_EXPERTS), lambda i: (layer, 0, 0))]
        args += [router_w, router_b.reshape(router_b.shape[0], 1, N_EXPERTS)]
        out_shape.append(jax.ShapeDtypeStruct((n, N_EXPERTS), F32))
        out_specs.append(pl.BlockSpec((tm, N_EXPERTS), lambda i: (i, 0)))
    res = pl.pallas_call(
        functools.partial(_norm_kernel, modulate=modulate, router=router),
        out_shape=out_shape,
        grid=(n // tm,),
        in_specs=in_specs,
        out_specs=out_specs,
        compiler_params=_params(("arbitrary",)),
        name="norm_modulate",
    )(*args)
    return res if router else res[0]


def _softmax_pv(s, v):
    m = jnp.max(s, axis=-1, keepdims=True)
    p = jnp.exp(s - m)
    p = p / jnp.sum(p, axis=-1, keepdims=True)
    return jnp.dot(p.astype(BF16), v, preferred_element_type=F32)


_NT = (((1,), (1,)), ((), ()))


def _gqa_kernel(q_ref, k_ref, v_ref, o_ref, *, scale):
    k = k_ref[...]
    v = v_ref[...]
    for h in range(GQA_GROUP):
        cols = slice(h * HEAD_DIM, (h + 1) * HEAD_DIM)
        s = lax.dot_general(q_ref[:, cols], k, _NT, preferred_element_type=F32) * scale
        o_ref[:, cols] = _softmax_pv(s, v).astype(o_ref.dtype)


def gqa_attention(q, k, v, tq=256):
    b, s, _ = q.shape
    sk = k.shape[1]
    gw = GQA_GROUP * HEAD_DIM
    return pl.pallas_call(
        functools.partial(_gqa_kernel, scale=HEAD_DIM ** -0.5),
        out_shape=jax.ShapeDtypeStruct(q.shape, BF16),
        grid=(b, GQA_KV_HEADS, s // tq),
        in_specs=[
            pl.BlockSpec((None, tq, gw), lambda bi, j, qi: (bi, qi, j)),
            pl.BlockSpec((None, sk, HEAD_DIM), lambda bi, j, qi: (bi, 0, j)),
            pl.BlockSpec((None, sk, HEAD_DIM), lambda bi, j, qi: (bi, 0, j)),
        ],
        out_specs=pl.BlockSpec((None, tq, gw), lambda bi, j, qi: (bi, qi, j)),
        compiler_params=_params(("arbitrary", "arbitrary", "arbitrary")),
        name="gqa_attention",
    )(q, k, v)


MLA_HEADS_PER_STEP = 2


def _mla_kernel(qn_ref, qp_ref, kv_ref, kpe_ref, o_ref, *, scale):
    kpe = kpe_ref[...]
    for h in range(MLA_HEADS_PER_STEP):
        kcols = slice(h * (MLA_NOPE + MLA_V), h * (MLA_NOPE + MLA_V) + MLA_NOPE)
        vcols = slice(h * (MLA_NOPE + MLA_V) + MLA_NOPE, (h + 1) * (MLA_NOPE + MLA_V))
        s = lax.dot_general(qn_ref[:, h * MLA_NOPE:(h + 1) * MLA_NOPE], kv_ref[:, kcols], _NT,
                            preferred_element_type=F32)
        s = s + lax.dot_general(qp_ref[:, h * MLA_ROPE:(h + 1) * MLA_ROPE], kpe, _NT,
                                preferred_element_type=F32)
        o_ref[:, h * MLA_V:(h + 1) * MLA_V] = _softmax_pv(s * scale, kv_ref[:, vcols]).astype(o_ref.dtype)


def mla_attention(q_nope, q_pe, kv, k_pe, tq=256):
    b, s, _ = q_nope.shape
    sk = kv.shape[1]
    hp = MLA_HEADS_PER_STEP
    return pl.pallas_call(
        functools.partial(_mla_kernel, scale=(MLA_NOPE + MLA_ROPE) ** -0.5),
        out_shape=jax.ShapeDtypeStruct((b, s, MLA_HEADS * MLA_V), BF16),
        grid=(b, MLA_HEADS // hp, s // tq),
        in_specs=[
            pl.BlockSpec((None, tq, hp * MLA_NOPE), lambda bi, g, qi: (bi, qi, g)),
            pl.BlockSpec((None, tq, hp * MLA_ROPE), lambda bi, g, qi: (bi, qi, g)),
            pl.BlockSpec((None, sk, hp * (MLA_NOPE + MLA_V)), lambda bi, g, qi: (bi, 0, g)),
            pl.BlockSpec((None, sk, MLA_ROPE), lambda bi, g, qi: (bi, 0, 0)),
        ],
        out_specs=pl.BlockSpec((None, tq, hp * MLA_V), lambda bi, g, qi: (bi, qi, g)),
        compiler_params=_params(("arbitrary", "arbitrary", "arbitrary")),
        name="mla_attention",
    )(q_nope, q_pe, kv, k_pe)


GDN_VREG_ROWS = 8


GDN_SOLVE_BLOCK = 16
GDN_HEADS_PER_STEP = 4


def _solve_unit_triangular(problems):
    n_rows = problems[0][0].shape[0]
    vr = GDN_VREG_ROWS
    n_groups = n_rows // vr
    gpb = GDN_SOLVE_BLOCK // vr
    n_blocks = n_rows // GDN_SOLVE_BLOCK
    all_groups = [[rhs[g * vr:(g + 1) * vr, :] for g in range(n_groups)] for _, rhs, _ in problems]
    for bi in range(n_blocks):
        for t in range(GDN_SOLVE_BLOCK):
            for (a, _, reverse), groups in zip(problems, all_groups):
                b = n_blocks - 1 - bi if reverse else bi
                g_lo, g_hi = b * gpb, (b + 1) * gpb
                j = g_hi * vr - 1 - t if reverse else g_lo * vr + t
                gj, sj = divmod(j, vr)
                xj = groups[gj][sj:sj + 1, :]
                for g in (range(gj, g_lo - 1, -1) if reverse else range(gj, g_hi)):
                    groups[g] = groups[g] - a[g * vr:(g + 1) * vr, j:j + 1] * xj
        if bi == n_blocks - 1:
            break
        for (a, _, reverse), groups in zip(problems, all_groups):
            b = n_blocks - 1 - bi if reverse else bi
            g_lo, g_hi = b * gpb, (b + 1) * gpb
            rest = range(0, g_lo) if reverse else range(g_hi, n_groups)
            zero = jnp.zeros_like(groups[0])
            x_blk = jnp.concatenate([groups[g] if g_lo <= g < g_hi else zero for g in range(n_groups)], axis=0)
            upd = jnp.dot(a[rest[0] * vr:(rest[-1] + 1) * vr, :], x_blk, precision=lax.Precision.HIGHEST,
                          preferred_element_type=F32)
            for t, g in enumerate(rest):
                groups[g] = groups[g] - upd[t * vr:(t + 1) * vr, :]
    return [jnp.concatenate(groups, axis=0) for groups in all_groups]


def _gdn_chunks(chunks):
    L = chunks[0][0].shape[0]
    ii = lax.broadcasted_iota(jnp.int32, (L, L), 0)
    jj = lax.broadcasted_iota(jnp.int32, (L, L), 1)
    pre = []
    for q, k, v, gc, gc_row, gc_end, beta, state, reverse in chunks:
        keep = (ii <= jj) if reverse else (ii >= jj)
        strict = (ii < jj) if reverse else (ii > jj)
        gamma = jnp.where(keep, jnp.exp(jnp.where(keep, gc - gc_row, 0.0)), 0.0)
        kb = k.astype(BF16)
        kk = lax.dot_general(kb, kb, _NT, preferred_element_type=F32)
        qk = lax.dot_general(q.astype(BF16), kb, _NT, preferred_element_type=F32)
        a = jnp.where(strict, kk * gamma * beta, 0.0)
        eg = jnp.exp(gc)
        pre.append((a, jnp.concatenate([v * beta, k * (beta * eg)], axis=1), (qk * gamma).astype(BF16), eg))
    xs = _solve_unit_triangular([(a, rhs, c[8]) for (a, rhs, _, _), c in zip(pre, chunks)])
    out = []
    for x, (_, _, qkg, eg), (q, k, v, gc, gc_row, gc_end, beta, state, reverse) in zip(xs, pre, chunks):
        dv = v.shape[1]
        u, w = x[:, :dv], x[:, dv:]
        ws = jnp.dot(jnp.concatenate([w, q * eg], axis=0).astype(BF16), state.astype(BF16),
                     preferred_element_type=F32)
        v_new = (u - ws[:L]).astype(BF16)
        o = ws[L:] + jnp.dot(qkg, v_new, preferred_element_type=F32)
        k_dec = (k * jnp.exp(gc_end - gc)).astype(BF16)
        new_state = state * jnp.exp(gc_end) + lax.dot_general(k_dec, v_new, (((0,), (0,)), ((), ())),
                                                              preferred_element_type=F32)
        out.append((o, new_state))
    return out


def _gdn_kernel(q_ref, k_ref, v_ref, z_ref, gcol_ref, grow_ref, s0_ref, nw_ref, o_ref, s_out_ref,
                of_scr, ob_scr, st_scr, *, n_chunks):
    L = GDN_CHUNK
    st_scr[...] = s0_ref[...]
    ii = lax.broadcasted_iota(jnp.int32, (L, L), 0)
    jj = lax.broadcasted_iota(jnp.int32, (L, L), 1)
    lower = (ii >= jj).astype(F32)
    upper = (ii <= jj).astype(F32)
    hi = lax.Precision.HIGHEST

    def chunk_step(c, carry):
        chunks, where = [], []
        for hh in range(GDN_HEADS_PER_STEP):
            cols = slice(hh * HEAD_DIM, (hh + 1) * HEAD_DIM)
            for d, reverse in ((0, False), (1, True)):
                cc = (n_chunks - 1 - c) if reverse else c
                r0 = pl.multiple_of(cc * L, L)
                q = q_ref[pl.ds(r0, L), cols]
                k = k_ref[pl.ds(r0, L), cols]
                v = v_ref[pl.ds(r0, L), cols]
                q = q * lax.rsqrt(jnp.sum(q * q, axis=-1, keepdims=True) + NORM_EPS) * (HEAD_DIM ** -0.5)
                k = k * lax.rsqrt(jnp.sum(k * k, axis=-1, keepdims=True) + NORM_EPS)
                gcol = gcol_ref[hh, cc]
                grow = grow_ref[hh, cc]
                csum = jnp.dot(upper if reverse else lower, gcol, precision=hi, preferred_element_type=F32)
                rsum = jnp.dot(grow, lower if reverse else upper, precision=hi, preferred_element_type=F32)
                gc = csum[:, d:d + 1]
                gc_end = gc[0:1, :] if reverse else gc[L - 1:L, :]
                chunks.append((q, k, v, gc, rsum[d:d + 1, :], gc_end, gcol[:, 2 + d:3 + d], st_scr[d, hh], reverse))
                where.append((d, hh, r0, cols))
        for (o, new_state), (d, hh, r0, cols) in zip(_gdn_chunks(chunks), where):
            st_scr[d, hh] = new_state
            (ob_scr if d else of_scr)[pl.ds(r0, L), cols] = o
        return carry

    lax.fori_loop(0, n_chunks, chunk_step, 0)
    s_out_ref[...] = st_scr[...]
    for hh in range(GDN_HEADS_PER_STEP):
        cols = slice(hh * HEAD_DIM, (hh + 1) * HEAD_DIM)
        o = of_scr[:, cols] + ob_scr[:, cols]
        o = o * lax.rsqrt(jnp.mean(o * o, axis=-1, keepdims=True) + NORM_EPS) * nw_ref[...]
        z = z_ref[:, cols]
        o_ref[:, cols] = (o * (z * (1.0 / (1.0 + jnp.exp(-z))))).astype(o_ref.dtype)


def gdn_mixer(qkv, proj, z_col0, g, beta, s0, norm_w, n_batch, seq, row0):
    L = GDN_CHUNK
    nc = seq // L
    gb = jnp.concatenate([g, beta], axis=2)
    gcol = jnp.transpose(gb, (0, 3, 1, 2)).reshape(n_batch, GDN_HEADS, nc, L, 4)
    grow = jnp.transpose(g, (0, 3, 2, 1)).reshape(n_batch, GDN_HEADS, 2, nc, L)
    grow = jnp.pad(jnp.swapaxes(grow, 2, 3), ((0, 0), (0, 0), (0, 0), (0, GDN_VREG_ROWS - 2), (0, 0)))
    rb = row0 // seq
    hps = GDN_HEADS_PER_STEP
    bw = hps * HEAD_DIM
    n_hb = GDN_HEADS // hps
    assert row0 % seq == 0 and z_col0 % bw == 0
    zb = z_col0 // bw
    o, s_out = pl.pallas_call(
        functools.partial(_gdn_kernel, n_chunks=nc),
        out_shape=[jax.ShapeDtypeStruct((n_batch * seq, GDN_WIDTH), BF16),
                   jax.ShapeDtypeStruct((n_batch, 2, GDN_HEADS, HEAD_DIM, HEAD_DIM), F32)],
        grid=(n_batch, n_hb),
        in_specs=[
            pl.BlockSpec((seq, bw), lambda b, h: (rb + b, h)),
            pl.BlockSpec((seq, bw), lambda b, h: (rb + b, n_hb + h)),
            pl.BlockSpec((seq, bw), lambda b, h: (rb + b, 2 * n_hb + h)),
            pl.BlockSpec((seq, bw), lambda b, h: (rb + b, zb + h)),
            pl.BlockSpec((None, hps, nc, L, 4), lambda b, h: (b, h, 0, 0, 0)),
            pl.BlockSpec((None, hps, nc, GDN_VREG_ROWS, L), lambda b, h: (b, h, 0, 0, 0)),
            pl.BlockSpec((None, 2, hps, HEAD_DIM, HEAD_DIM), lambda b, h: (b, 0, h, 0, 0)),
            pl.BlockSpec((1, HEAD_DIM), lambda b, h: (0, 0)),
        ],
        out_specs=[
            pl.BlockSpec((seq, bw), lambda b, h: (b, h)),
            pl.BlockSpec((None, 2, hps, HEAD_DIM, HEAD_DIM), lambda b, h: (b, 0, h, 0, 0)),
        ],
        scratch_shapes=[pltpu.VMEM((seq, bw), F32), pltpu.VMEM((seq, bw), F32),
                        pltpu.VMEM((2, hps, HEAD_DIM, HEAD_DIM), F32)],
        compiler_params=_params(("arbitrary", "arbitrary")),
        name="gdn_mixer",
    )(qkv, qkv, qkv, proj, gcol, grow, s0, norm_w.reshape(1, HEAD_DIM))
    return o, s_out


SSD_HEADS_PER_GROUP = SSD_HEADS // SSD_GROUPS
SSD_GROUP_WIDTH = SSD_HEADS_PER_GROUP * SSD_HEAD_DIM


def _ssd_head_chunks(problems):
    L = problems[0][0].shape[0]
    ii = lax.broadcasted_iota(jnp.int32, (L, L), 0)
    jj = lax.broadcasted_iota(jnp.int32, (L, L), 1)
    tn = (((0,), (0,)), ((), ()))
    y_intra, y_inter, upd = [], [], []
    for xdt, cb, bm, cm, cs, cs_row, cs_end, state, reverse in problems:
        keep = (ii <= jj) if reverse else (ii >= jj)
        decay = jnp.where(keep, jnp.exp(jnp.where(keep, cs - cs_row, 0.0)), 0.0)
        y_intra.append(jnp.dot((cb * decay).astype(BF16), xdt, preferred_element_type=F32))
    for xdt, cb, bm, cm, cs, cs_row, cs_end, state, reverse in problems:
        y_inter.append(lax.dot_general((cm * jnp.exp(cs)).astype(BF16), state.astype(BF16), _NT,
                                       preferred_element_type=F32))
    for xdt, cb, bm, cm, cs, cs_row, cs_end, state, reverse in problems:
        b_dec = (bm * jnp.exp(cs_end - cs)).astype(BF16)
        upd.append(lax.dot_general(xdt, b_dec, tn, preferred_element_type=F32))
    return [(ya + yb, p[7] * jnp.exp(p[6]) + u) for ya, yb, u, p in zip(y_intra, y_inter, upd, problems)]


def _ssd_kernel(x_ref, b_ref, c_ref, dtc_ref, dac_ref, dar_ref, h0_ref, dsk_ref, y_ref, h_out_ref,
                yf_scr, yb_scr, st_scr, *, n_chunks):
    L = SSD_CHUNK
    nh = SSD_HEADS_PER_GROUP
    st_scr[...] = h0_ref[...]
    ii = lax.broadcasted_iota(jnp.int32, (L, L), 0)
    jj = lax.broadcasted_iota(jnp.int32, (L, L), 1)
    lower = (ii >= jj).astype(F32)
    upper = (ii <= jj).astype(F32)
    hi = lax.Precision.HIGHEST

    def chunk_step(c, carry):
        problems, rows = [], []
        for d, reverse in ((0, False), (1, True)):
            cc = (n_chunks - 1 - c) if reverse else c
            r0 = pl.multiple_of(cc * L, L)
            x = x_ref[pl.ds(r0, L), :]
            bm = b_ref[pl.ds(r0, L), :]
            cm = c_ref[pl.ds(r0, L), :]
            cb = lax.dot_general(cm.astype(BF16), bm.astype(BF16), _NT, preferred_element_type=F32)
            dtc = dtc_ref[cc]
            csum = jnp.dot(upper if reverse else lower, dac_ref[cc], precision=hi, preferred_element_type=F32)
            rsum = jnp.dot(dar_ref[cc], lower if reverse else upper, precision=hi, preferred_element_type=F32)
            rows.append(r0)
            for j in range(nh):
                col = d * nh + j
                xdt = (x[:, j * SSD_HEAD_DIM:(j + 1) * SSD_HEAD_DIM] * dtc[:, col:col + 1]).astype(BF16)
                cs = csum[:, col:col + 1]
                cs_end = cs[0:1, :] if reverse else cs[L - 1:L, :]
                problems.append((xdt, cb, bm, cm, cs, rsum[col:col + 1, :], cs_end, st_scr[d, j], reverse))
        results = _ssd_head_chunks(problems)
        for d in range(2):
            for j in range(nh):
                st_scr[d, j] = results[d * nh + j][1]
            ys = jnp.concatenate([results[d * nh + j][0] for j in range(nh)], axis=1)
            (yb_scr if d else yf_scr)[pl.ds(rows[d], L), :] = ys
        return carry

    lax.fori_loop(0, n_chunks, chunk_step, 0)
    h_out_ref[...] = st_scr[...]
    y_ref[...] = yf_scr[...] + yb_scr[...] + dsk_ref[...] * x_ref[...]


def ssd_mixer(xbc, dt, a, d_skip, h0, n_batch, seq, row0):
    L = SSD_CHUNK
    nc = seq // L
    nh = SSD_HEADS_PER_GROUP

    def group_cols(t):
        t = t.reshape(n_batch, seq, 2, SSD_GROUPS, nh)
        return jnp.transpose(t, (0, 3, 1, 2, 4)).reshape(n_batch, SSD_GROUPS, nc, L, 2 * nh)

    dtc = group_cols(dt)
    dac = group_cols(dt * a)
    dar = jnp.swapaxes(dac, 3, 4)
    dsk = jnp.repeat(jnp.sum(d_skip, axis=0), SSD_HEAD_DIM).reshape(1, SSD_INNER)
    rb = row0 // seq
    assert row0 % seq == 0
    gw = SSD_GROUP_WIDTH
    b_blk0 = SSD_INNER // SSD_STATE
    c_blk0 = b_blk0 + SSD_GROUPS
    y, h_out = pl.pallas_call(
        functools.partial(_ssd_kernel, n_chunks=nc),
        out_shape=[jax.ShapeDtypeStruct((n_batch * seq, SSD_INNER), F32),
                   jax.ShapeDtypeStruct((n_batch, 2, SSD_HEADS, SSD_HEAD_DIM, SSD_STATE), F32)],
        grid=(n_batch, SSD_GROUPS),
        in_specs=[
            pl.BlockSpec((seq, gw), lambda b, g: (rb + b, g)),
            pl.BlockSpec((seq, SSD_STATE), lambda b, g: (rb + b, b_blk0 + g)),
            pl.BlockSpec((seq, SSD_STATE), lambda b, g: (rb + b, c_blk0 + g)),
            pl.BlockSpec((None, None, nc, L, 2 * nh), lambda b, g: (b, g, 0, 0, 0)),
            pl.BlockSpec((None, None, nc, L, 2 * nh), lambda b, g: (b, g, 0, 0, 0)),
            pl.BlockSpec((None, None, nc, 2 * nh, L), lambda b, g: (b, g, 0, 0, 0)),
            pl.BlockSpec((None, 2, nh, SSD_HEAD_DIM, SSD_STATE), lambda b, g: (b, 0, g, 0, 0)),
            pl.BlockSpec((1, gw), lambda b, g: (0, g)),
        ],
        out_specs=[
            pl.BlockSpec((seq, gw), lambda b, g: (b, g)),
            pl.BlockSpec((None, 2, nh, SSD_HEAD_DIM, SSD_STATE), lambda b, g: (b, 0, g, 0, 0)),
        ],
        scratch_shapes=[pltpu.VMEM((seq, gw), F32), pltpu.VMEM((seq, gw), F32),
                        pltpu.VMEM((2, nh, SSD_HEAD_DIM, SSD_STATE), F32)],
        compiler_params=_params(("arbitrary", "arbitrary")),
        name="ssd_mixer",
    )(xbc, xbc, xbc, dtc, dac, dar, h0, dsk)
    return y, h_out


def _gated_norm_kernel(y_ref, z0_ref, z1_ref, w_ref, o_ref):
    z = jnp.concatenate([z0_ref[...], z1_ref[...]], axis=1)
    t = y_ref[...] * (z * (1.0 / (1.0 + jnp.exp(-z))))
    t = t * lax.rsqrt(jnp.mean(t * t, axis=-1, keepdims=True) + NORM_EPS) * w_ref[...]
    o_ref[...] = t.astype(o_ref.dtype)


def ssd_gated_norm(y, proj, z_col0, norm_w):
    n, d = y.shape
    tm = ROW_TILE
    half = d // 2
    assert z_col0 % half == 0
    zb = z_col0 // half
    return pl.pallas_call(
        _gated_norm_kernel,
        out_shape=jax.ShapeDtypeStruct((n, d), BF16),
        grid=(n // tm,),
        in_specs=[pl.BlockSpec((tm, d), lambda i: (i, 0)),
                  pl.BlockSpec((tm, half), lambda i: (i, zb)),
                  pl.BlockSpec((tm, half), lambda i: (i, zb + 1)),
                  pl.BlockSpec((1, d), lambda i: (0, 0))],
        out_specs=pl.BlockSpec((tm, d), lambda i: (i, 0)),
        compiler_params=_params(("arbitrary",)),
        name="ssd_gated_norm",
    )(y, proj, proj, norm_w.reshape(1, d))


ITEM_VALID = 1
ITEM_NEW_WEIGHTS = 2


def _moe_up_kernel(ib_ref, ic_ref, iwe_ref, iwc_ref, iflag_ref, x_ref, wg_ref, wu_ref, bg_ref, bu_ref, o_ref,
                   wg_bf, wu_bf):
    flag = iflag_ref[pl.program_id(0)]

    @pl.when((flag & ITEM_NEW_WEIGHTS) != 0)
    def _():
        wg_bf[...] = wg_ref[...].astype(BF16)
        wu_bf[...] = wu_ref[...].astype(BF16)

    @pl.when((flag & ITEM_VALID) != 0)
    def _():
        x = x_ref[...]
        gate = jnp.minimum(jnp.dot(x, wg_bf[...], preferred_element_type=F32) + bg_ref[...], SWIGLU_LIMIT)
        up = jnp.clip(jnp.dot(x, wu_bf[...], preferred_element_type=F32) + bu_ref[...], -SWIGLU_LIMIT, SWIGLU_LIMIT)
        act = (up + 1.0) * gate * (1.0 / (1.0 + jnp.exp(-SWIGLU_ALPHA * gate)))
        o_ref[...] = act.astype(o_ref.dtype)

    @pl.when((flag & ITEM_VALID) == 0)
    def _():
        o_ref[...] = jnp.zeros_like(o_ref)


def _moe_down_kernel(ib_ref, ic_ref, iwe_ref, iwc_ref, iflag_ref, a_ref, wd_ref, bd_ref, o_ref, wd_bf):
    flag = iflag_ref[pl.program_id(0)]

    @pl.when((flag & ITEM_NEW_WEIGHTS) != 0)
    def _():
        wd_bf[...] = wd_ref[...].astype(BF16)

    @pl.when((flag & ITEM_VALID) != 0)
    def _():
        o_ref[...] = jnp.dot(a_ref[...], wd_bf[...], preferred_element_type=F32) + bd_ref[...]

    @pl.when((flag & ITEM_VALID) == 0)
    def _():
        o_ref[...] = jnp.zeros_like(o_ref)


def _moe_items(blk_start, nblk, n_used, n_chunks):
    n_items = MOE_NBLK * n_chunks
    i = jnp.arange(n_items, dtype=jnp.int32)
    item_end = (blk_start + nblk) * n_chunks
    e = jnp.minimum(jnp.searchsorted(item_end, i, side='right'), N_EXPERTS - 1).astype(jnp.int32)
    r = i - blk_start[e] * n_chunks
    nb = jnp.maximum(nblk[e], 1)
    valid = i < n_used * n_chunks
    n_tail = jnp.maximum(MOE_NBLK - n_used, 1)
    rt = i - n_used * n_chunks
    chunk = jnp.where(valid, r // nb, rt // n_tail).astype(jnp.int32)
    blk = jnp.where(valid, blk_start[e] + r % nb, n_used + rt % n_tail).astype(jnp.int32)
    last = jnp.maximum(n_used * n_chunks - 1, 0)
    we = jnp.where(valid, e, e[last]).astype(jnp.int32)
    wc = jnp.where(valid, chunk, chunk[last]).astype(jnp.int32)
    new_w = jnp.concatenate([jnp.ones((1,), bool), (we[1:] != we[:-1]) | (wc[1:] != wc[:-1])])
    flag = valid.astype(jnp.int32) * ITEM_VALID + new_w.astype(jnp.int32) * ITEM_NEW_WEIGHTS
    return blk, chunk, we, wc, flag


MOE_COMBINE_TT = 128


MOE_COMBINE_UNROLL = 8
MOE_COMBINE_ROW_GROUP = 8


def _combine_copy(y_hbm, buf, sem, slot, k, src_row, t, n_rows):
    return pltpu.make_async_copy(y_hbm.at[pl.ds(src_row, n_rows), :], buf.at[slot, k, pl.ds(t, n_rows), :],
                                 sem.at[slot])


def _moe_combine_kernel(pos_ref, pos_next_ref, y_hbm, rw_ref, x_ref, gate_ref, o_ref, buf, sem):
    i = pl.program_id(0)
    n = pl.num_programs(0)
    tt = MOE_COMBINE_TT

    def start_all(p_ref, slot):
        for k in range(TOP_K):
            def body(t, carry):
                _combine_copy(y_hbm, buf, sem, slot, k, p_ref[k, t], t, 1).start()
                return carry
            lax.fori_loop(0, tt, body, 0, unroll=MOE_COMBINE_UNROLL)

    @pl.when(i == 0)
    def _():
        start_all(pos_ref, 0)

    @pl.when(i + 1 < n)
    def _():
        start_all(pos_next_ref, (i + 1) % 2)

    slot = i % 2
    for k in range(TOP_K):
        _combine_copy(y_hbm, buf, sem, slot, k, 0, 0, tt).wait()

    rg = MOE_COMBINE_ROW_GROUP

    def sum_rows(g, carry):
        rows = pl.ds(pl.multiple_of(g * rg, rg), rg)
        rw = rw_ref[rows, :]
        ffn = buf[slot, 0, rows, :] * rw[:, 0:1]
        for k in range(1, TOP_K):
            ffn = ffn + buf[slot, k, rows, :] * rw[:, k:k + 1]
        o_ref[rows, :] = x_ref[rows, :] + gate_ref[...] * ffn
        return carry
    lax.fori_loop(0, tt // rg, sum_rows, 0)


def moe_combine(y, slot_of_assign, router_weights, x, gate):
    t, d = x.shape
    tt = MOE_COMBINE_TT
    n_steps = t // tt
    pos = jnp.swapaxes(slot_of_assign.reshape(n_steps, tt, TOP_K), 1, 2)
    smem_spec = functools.partial(pl.BlockSpec, (None, TOP_K, tt), memory_space=pltpu.SMEM)
    return pl.pallas_call(
        _moe_combine_kernel,
        out_shape=jax.ShapeDtypeStruct((t, d), F32),
        grid=(n_steps,),
        in_specs=[
            smem_spec(lambda i: (i, 0, 0)),
            smem_spec(lambda i: (jnp.minimum(i + 1, n_steps - 1), 0, 0)),
            pl.BlockSpec(memory_space=pl.ANY),
            pl.BlockSpec((tt, TOP_K), lambda i: (i, 0)),
            pl.BlockSpec((tt, d), lambda i: (i, 0)),
            pl.BlockSpec((None, 1, d), lambda i: (_cond_row_of_tile(i, tt), 0, 0)),
        ],
        out_specs=pl.BlockSpec((tt, d), lambda i: (i, 0)),
        scratch_shapes=[pltpu.VMEM((2, TOP_K, tt, d), F32), pltpu.SemaphoreType.DMA((2,))],
        compiler_params=_params(("arbitrary",)),
        name="moe_combine",
    )(pos, pos, y, router_weights, x, gate)


def moe_ffn(h, logits, layer, w_gate, b_gate, w_up, b_up, w_down, b_down, x, gate):
    t, d = h.shape
    n_assign = t * TOP_K
    top_logit, top_idx = lax.top_k(logits, TOP_K)
    gates = jax.nn.softmax(top_logit, axis=-1)
    flat_e = top_idx.reshape(n_assign).astype(jnp.int32)
    rb = 128
    one_hot = (flat_e[:, None] == jnp.arange(N_EXPERTS, dtype=jnp.int32)).astype(F32).reshape(n_assign // rb, rb,
                                                                                              N_EXPERTS)
    within = jnp.einsum('ij,bjk->bik', jnp.tril(jnp.ones((rb, rb), F32)), one_hot)
    block_tot = within[:, -1, :]
    before = jnp.cumsum(block_tot, axis=0) - block_tot
    rank = (jnp.sum(one_hot * (within + before[:, None, :]), axis=-1) - 1.0).reshape(n_assign).astype(jnp.int32)
    counts = jnp.sum(block_tot, axis=0).astype(jnp.int32)
    nblk = (counts + MOE_TM - 1) // MOE_TM
    blk_end = jnp.cumsum(nblk).astype(jnp.int32)
    blk_start = blk_end - nblk
    n_used = blk_end[-1]
    sort_start = jnp.cumsum(counts).astype(jnp.int32) - counts
    slot_of_assign = (blk_start[flat_e] * MOE_TM + rank).reshape(t, TOP_K)
    order = jnp.argsort(flat_e).astype(jnp.int32)
    slot = jnp.arange(MOE_SLOTS, dtype=jnp.int32)
    slot_e = jnp.minimum(jnp.searchsorted(blk_end, slot // MOE_TM, side='right'), N_EXPERTS - 1).astype(jnp.int32)
    off = slot - blk_start[slot_e] * MOE_TM
    valid = (slot // MOE_TM < n_used) & (off < counts[slot_e])
    slot_tok = jnp.where(valid, order[jnp.clip(sort_start[slot_e] + off, 0, n_assign - 1)] // TOP_K, t)
    x_slots = jnp.concatenate([h, jnp.zeros((1, d), h.dtype)], axis=0)[slot_tok]

    depth = w_gate.shape[0]
    n_fc = EXPERT_FF // MOE_TF
    items = _moe_items(blk_start, nblk, n_used, n_fc)
    act = pl.pallas_call(
        _moe_up_kernel,
        out_shape=jax.ShapeDtypeStruct((MOE_SLOTS, EXPERT_FF), BF16),
        grid_spec=pltpu.PrefetchScalarGridSpec(
            num_scalar_prefetch=5,
            grid=(MOE_NBLK * n_fc,),
            in_specs=[
                pl.BlockSpec((MOE_TM, d), lambda i, ib, ic, iwe, iwc, ifl: (ib[i], 0)),
                pl.BlockSpec((None, None, d, MOE_TF), lambda i, ib, ic, iwe, iwc, ifl: (layer, iwe[i], 0, iwc[i])),
                pl.BlockSpec((None, None, d, MOE_TF), lambda i, ib, ic, iwe, iwc, ifl: (layer, iwe[i], 0, iwc[i])),
                pl.BlockSpec((None, None, 1, MOE_TF), lambda i, ib, ic, iwe, iwc, ifl: (layer, iwe[i], 0, iwc[i])),
                pl.BlockSpec((None, None, 1, MOE_TF), lambda i, ib, ic, iwe, iwc, ifl: (layer, iwe[i], 0, iwc[i])),
            ],
            out_specs=pl.BlockSpec((MOE_TM, MOE_TF), lambda i, ib, ic, iwe, iwc, ifl: (ib[i], ic[i])),
            scratch_shapes=[pltpu.VMEM((d, MOE_TF), BF16), pltpu.VMEM((d, MOE_TF), BF16)],
        ),
        compiler_params=_params(("arbitrary",)),
        name="moe_gate_up",
    )(*items, x_slots, w_gate, w_up,
      b_gate.reshape(depth, N_EXPERTS, 1, EXPERT_FF), b_up.reshape(depth, N_EXPERTS, 1, EXPERT_FF))

    n_nc = d // MOE_TN
    items = _moe_items(blk_start, nblk, n_used, n_nc)
    y = pl.pallas_call(
        _moe_down_kernel,
        out_shape=jax.ShapeDtypeStruct((MOE_SLOTS, d), F32),
        grid_spec=pltpu.PrefetchScalarGridSpec(
            num_scalar_prefetch=5,
            grid=(MOE_NBLK * n_nc,),
            in_specs=[
                pl.BlockSpec((MOE_TM, EXPERT_FF), lambda i, ib, ic, iwe, iwc, ifl: (ib[i], 0)),
                pl.BlockSpec((None, None, EXPERT_FF, MOE_TN),
                             lambda i, ib, ic, iwe, iwc, ifl: (layer, iwe[i], 0, iwc[i])),
                pl.BlockSpec((None, None, 1, MOE_TN), lambda i, ib, ic, iwe, iwc, ifl: (layer, iwe[i], 0, iwc[i])),
            ],
            out_specs=pl.BlockSpec((MOE_TM, MOE_TN), lambda i, ib, ic, iwe, iwc, ifl: (ib[i], ic[i])),
            scratch_shapes=[pltpu.VMEM((EXPERT_FF, MOE_TN), BF16)],
        ),
        compiler_params=_params(("arbitrary",)),
        name="moe_down",
    )(*items, act, w_down, b_down.reshape(depth, N_EXPERTS, 1, d))
    return moe_combine(y, slot_of_assign, gates, x, gate)


def _split(x, sizes):
    return jnp.split(x, np.cumsum(sizes)[:-1].tolist(), axis=-1)


def _rms(x, w):
    return (x * lax.rsqrt(jnp.mean(x * x, axis=-1, keepdims=True) + NORM_EPS)) * w


def _rope_1d(x, pos):
    half = x.shape[-1] // 2
    inv_freq = ROPE_BASE ** (-jnp.arange(half, dtype=F32) / half)
    ang = pos.astype(F32)[:, None] * inv_freq
    cos = jnp.cos(ang)[:, None, :]
    sin = jnp.sin(ang)[:, None, :]
    x1, x2 = x[..., :half], x[..., half:]
    return jnp.concatenate([x1 * cos - x2 * sin, x2 * cos + x1 * sin], axis=-1)


def _rope_2d(x, n_rows):
    t = jnp.arange(n_rows * GRID_W)
    d_axis = x.shape[-1] // 2
    return jnp.concatenate([_rope_1d(x[..., :d_axis], t // GRID_W), _rope_1d(x[..., d_axis:], t % GRID_W)], axis=-1)


def _short_conv(x, w):
    pad = (w.shape[0] - 1) // 2
    s = x.shape[1]
    xp = jnp.pad(x, ((0, 0), (pad, pad), (0, 0)))
    out = xp[:, 0:s] * w[0]
    for j in range(1, w.shape[0]):
        out = out + xp[:, j:j + s] * w[j]
    return out


def _conv_silu_merged(cols, conv_w, conv_b=None):
    outs = []
    for t in (cols[:N_PROMPT].reshape(BATCH, SEQ, -1), cols[N_PROMPT:].reshape(DEC_BATCH, DEC_SEQ, -1)):
        t = _short_conv(t, conv_w)
        if conv_b is not None:
            t = t + conv_b
        outs.append(jax.nn.silu(t).reshape(-1, t.shape[-1]))
    return jnp.concatenate(outs, axis=0)


def _even_attention_pass(proj, q_norm, k_norm, ctx, n_rows):
    b, s, _ = proj.shape
    q, k, v = _split(proj[..., :sum(EVEN_SPLITS[:3])], EVEN_SPLITS[:3])
    q = _rms(q.reshape(b, s, GQA_HEADS, HEAD_DIM), q_norm)
    k = _rms(k.reshape(b, s, GQA_KV_HEADS, HEAD_DIM), k_norm)
    v = v.reshape(b, s, GQA_KV_HEADS, HEAD_DIM)
    if ctx is None:
        k_all, v_all = k, v
    else:
        k_ctx, v_ctx = ctx
        q = _rope_2d(q, n_rows)
        k_all = jnp.concatenate([k_ctx, _rope_2d(k, n_rows)], axis=1)
        v_all = jnp.concatenate([v_ctx, v], axis=1)
    sk = k_all.shape[1]
    attn = gqa_attention(q.reshape(b, s, -1).astype(BF16), k_all.reshape(b, sk, -1).astype(BF16),
                         v_all.reshape(b, sk, -1).astype(BF16))
    return attn, k, v


def kernel(x_prompt, x_sample, cache_gqa_k, cache_gqa_v, state_ssd, state_gdn, cache_mla_ckv, cache_mla_kpe, c, c_ctx, w_mod, b_mod, norm_mix, norm_ffn, ev_w_in, ev_w_out, gqa_q_norm, gqa_k_norm, ssd_conv_w, ssd_conv_b, ssd_a_log, ssd_dt_bias, ssd_d_skip, ssd_norm, od_w_in, od_w_out, gdn_conv_w, gdn_a_log, gdn_dt_bias, gdn_norm, mla_q_a_norm, mla_w_q_b, mla_kv_a_norm, mla_w_kv_b, router_w, router_b, moe_w_gate, moe_b_gate, moe_w_up, moe_b_up, moe_w_down, moe_b_down, final_norm):
    n_rows = DEC_SEQ // GRID_W
    x = jnp.concatenate([x_prompt.reshape(N_PROMPT, D_MODEL), x_sample.reshape(N_SAMPLE, D_MODEL)], axis=0)

    cond = jnp.concatenate([c_ctx[None, :], c, jnp.zeros((N_COND_PAD - N_COND, D_MODEL), F32)], axis=0)
    mods = modulation(jax.nn.silu(cond).astype(BF16), w_mod, b_mod)
    mods = mods.reshape(DEPTH, N_COND_PAD, N_MOD, 1, D_MODEL)

    def split_passes(t):
        return (t[:N_PROMPT].reshape(BATCH, SEQ, -1), t[N_PROMPT:].reshape(DEC_BATCH, DEC_SEQ, -1))

    def merge_passes(tp, ts):
        return jnp.concatenate([tp.reshape(N_PROMPT, -1), ts.reshape(N_SAMPLE, -1)], axis=0)

    ctx_even, ctx_odd = [], []
    for l in range(DEPTH):
        i = l // 2
        sh1, sc1, g1, sh2, sc2, g2 = (mods[l, :, j] for j in range(N_MOD))
        h = norm_modulate(x, norm_mix, l, sc1, sh1)
        if l % 2 == 0:
            proj = matmul(h, ev_w_in, (i,), n_cols=EVEN_MAIN, name="even_in_proj")
            dt_proj = matmul(h, ev_w_in[i, :, EVEN_MAIN:], name="even_dt_proj")
            pp, ps = split_passes(proj)
            attn_p, k_p, v_p = _even_attention_pass(pp, gqa_q_norm[i], gqa_k_norm[i], None, None)
            attn_s, _, _ = _even_attention_pass(ps, gqa_q_norm[i], gqa_k_norm[i],
                                                (cache_gqa_k[:, i], cache_gqa_v[:, i]), n_rows)
            xbc = _conv_silu_merged(proj[:, EVEN_MAIN - SSD_CONV_CH:], ssd_conv_w[i], ssd_conv_b[i])
            dt = jax.nn.softplus(dt_proj.reshape(N_TOK, 2, SSD_HEADS) + ssd_dt_bias[i])
            dt_p, dt_s = dt[:N_PROMPT].reshape(BATCH, SEQ, 2, SSD_HEADS), dt[N_PROMPT:].reshape(
                DEC_BATCH, DEC_SEQ, 2, SSD_HEADS)
            a = -jnp.exp(ssd_a_log[i])
            y_p, h_p = ssd_mixer(xbc, dt_p, a, ssd_d_skip[i],
                                 jnp.zeros((BATCH, 2, SSD_HEADS, SSD_HEAD_DIM, SSD_STATE), F32), BATCH, SEQ, 0)
            y_s, _ = ssd_mixer(xbc, dt_s, a, ssd_d_skip[i], state_ssd[:, i], DEC_BATCH, DEC_SEQ, N_PROMPT)
            yn = ssd_gated_norm(jnp.concatenate([y_p, y_s], axis=0), proj, sum(EVEN_SPLITS[:3]), ssd_norm[i])
            ctx_even.append((k_p, v_p, h_p))
            feat = jnp.concatenate([merge_passes(attn_p, attn_s), yn], axis=-1)
            x = matmul(feat, ev_w_out, (i,), residual=x, gate=g1, name="even_out_proj")
        else:
            proj = matmul(h, od_w_in, (i,), tn=896, name="odd_in_proj")
            off = np.cumsum((0,) + ODD_SPLITS)
            qkv = _conv_silu_merged(proj[:, :off[1]], gdn_conv_w[i])
            gates = proj[:, off[2]:off[3]].reshape(N_TOK, 2, 2, GDN_HEADS)
            g = -jnp.exp(gdn_a_log[i]) * jax.nn.softplus(gates[:, 0] + gdn_dt_bias[i])
            beta = jax.nn.sigmoid(gates[:, 1])

            def gate_passes(t):
                return (t[:N_PROMPT].reshape(BATCH, SEQ, 2, GDN_HEADS),
                        t[N_PROMPT:].reshape(DEC_BATCH, DEC_SEQ, 2, GDN_HEADS))

            (g_p, g_s), (beta_p, beta_s) = gate_passes(g), gate_passes(beta)
            o_p, s_p = gdn_mixer(qkv, proj, int(off[1]), g_p, beta_p,
                                 jnp.zeros((BATCH, 2, GDN_HEADS, HEAD_DIM, HEAD_DIM), F32), gdn_norm[i], BATCH, SEQ, 0)
            o_s, _ = gdn_mixer(qkv, proj, int(off[1]), g_s, beta_s, state_gdn[:, i], gdn_norm[i], DEC_BATCH, DEC_SEQ,
                               N_PROMPT)
            o_p = o_p.reshape(BATCH, SEQ, -1)
            o_s = o_s.reshape(DEC_BATCH, DEC_SEQ, -1)
            qa = _rms(proj[:, off[3]:off[4]], mla_q_a_norm[i]).astype(BF16)
            ckv = _rms(proj[:, off[4]:off[5]], mla_kv_a_norm[i])
            ckv_p, ckv_s = split_passes(ckv)
            kpe_p, kpe_s = split_passes(proj[:, off[5]:off[6]])
            ctx_odd.append((s_p, ckv_p, kpe_p))
            qm = matmul(qa, mla_w_q_b, (i,), name="mla_q_b")
            ckv_all_s = jnp.concatenate([cache_mla_ckv[:, i], ckv_s], axis=1)
            sk_s = PAST_LEN + DEC_SEQ
            kv = matmul(jnp.concatenate([ckv_p.reshape(N_PROMPT, -1), ckv_all_s.reshape(DEC_BATCH * sk_s, -1)],
                                        axis=0).astype(BF16),
                        mla_w_kv_b, (i,), out_dtype=BF16, name="mla_kv_b")
            kv_p = kv[:N_PROMPT].reshape(BATCH, SEQ, -1)
            kv_s = kv[N_PROMPT:].reshape(DEC_BATCH, sk_s, -1)
            qm_p, qm_s = split_passes(qm)
            qm_p = qm_p.reshape(BATCH, SEQ, MLA_HEADS, MLA_NOPE + MLA_ROPE)
            qm_s = qm_s.reshape(DEC_BATCH, DEC_SEQ, MLA_HEADS, MLA_NOPE + MLA_ROPE)
            qpe_s = _rope_2d(qm_s[..., MLA_NOPE:], n_rows)
            kpe_all_s = jnp.concatenate([cache_mla_kpe[:, i], _rope_2d(kpe_s[:, :, None, :], n_rows)[:, :, 0]],
                                        axis=1)
            om_p = mla_attention(qm_p[..., :MLA_NOPE].reshape(BATCH, SEQ, -1).astype(BF16),
                                 qm_p[..., MLA_NOPE:].reshape(BATCH, SEQ, -1).astype(BF16), kv_p,
                                 kpe_p.astype(BF16))
            om_s = mla_attention(qm_s[..., :MLA_NOPE].reshape(DEC_BATCH, DEC_SEQ, -1).astype(BF16),
                                 qpe_s.reshape(DEC_BATCH, DEC_SEQ, -1).astype(BF16), kv_s, kpe_all_s.astype(BF16))
            feat = merge_passes(jnp.concatenate([o_p, om_p], axis=-1), jnp.concatenate([o_s, om_s], axis=-1))
            x = matmul(feat, od_w_out, (i,), residual=x, gate=g1, name="odd_out_proj")
        h, logits = norm_modulate(x, norm_ffn, l, sc2, sh2, router_w, router_b)
        x = moe_ffn(h, logits, l, moe_w_gate, moe_b_gate, moe_w_up, moe_b_up, moe_w_down, moe_b_down, x, g2)

    y = norm_modulate(x, final_norm[None, :], 0, out_dtype=F32)
    y_prompt = y[:N_PROMPT].reshape(BATCH, SEQ, D_MODEL)
    y_sample = y[N_PROMPT:].reshape(DEC_BATCH, DEC_SEQ, D_MODEL)
    new_gqa_k = jnp.stack([t[0] for t in ctx_even], axis=1)
    new_gqa_v = jnp.stack([t[1] for t in ctx_even], axis=1)
    new_ssd = jnp.stack([t[2] for t in ctx_even], axis=1)
    new_gdn = jnp.stack([t[0] for t in ctx_odd], axis=1)
    new_mla_ckv = jnp.stack([t[1] for t in ctx_odd], axis=1)
    new_mla_kpe = jnp.stack([t[2] for t in ctx_odd], axis=1)
    return (y_prompt, y_sample, new_gqa_k, new_gqa_v, new_ssd, new_gdn, new_mla_ckv, new_mla_kpe)
```

```python
import functools
import math

import jax
import jax.numpy as jnp
import numpy as np
from jax import lax
from jax.experimental import pallas as pl
from jax.experimental.pallas import tpu as pltpu

F32 = jnp.float32
BF16 = jnp.bfloat16

D_MODEL = 4096
BATCH = 16
SEQ = 256
DEPTH = 4
DEC_BATCH = 8
DEC_SEQ = 1024
PAST_LEN = 256
GRID_W = 64
HEAD_DIM = 128
ROPE_BASE = 10000.0
NORM_EPS = 1e-6
CONV_WIDTH = 5
GROUP_WIDTH = D_MODEL // 2
N_MOD = 6
GQA_HEADS = GROUP_WIDTH // HEAD_DIM
GQA_KV_HEADS = GQA_HEADS // 4
GQA_GROUP = GQA_HEADS // GQA_KV_HEADS
SSD_INNER = GROUP_WIDTH
SSD_HEAD_DIM = 64
SSD_HEADS = SSD_INNER // SSD_HEAD_DIM
SSD_GROUPS = 4
SSD_STATE = 128
SSD_CHUNK = 128
SSD_CONV_CH = SSD_INNER + 2 * SSD_GROUPS * SSD_STATE
GDN_HEADS = GROUP_WIDTH // HEAD_DIM
GDN_WIDTH = GDN_HEADS * HEAD_DIM
GDN_CHUNK = 64
MLA_HEADS = GROUP_WIDTH // HEAD_DIM
MLA_Q_LORA = D_MODEL // 4
MLA_KV_LORA = 512
MLA_NOPE = 128
MLA_ROPE = 64
MLA_V = 128
N_EXPERTS = 32
TOP_K = 4
EXPERT_FF = 1024
SWIGLU_LIMIT = 7.0
SWIGLU_ALPHA = 1.702
EVEN_SPLITS = (GQA_HEADS * HEAD_DIM, GQA_KV_HEADS * HEAD_DIM, GQA_KV_HEADS * HEAD_DIM, SSD_INNER, SSD_CONV_CH,
               2 * SSD_HEADS)
ODD_SPLITS = (3 * GDN_WIDTH, GDN_WIDTH, 4 * GDN_HEADS, MLA_Q_LORA, MLA_KV_LORA, MLA_ROPE)
EVEN_MAIN = sum(EVEN_SPLITS[:-1])

N_PROMPT = BATCH * SEQ
N_SAMPLE = DEC_BATCH * DEC_SEQ
N_TOK = N_PROMPT + N_SAMPLE
N_COND = 1 + DEC_BATCH
N_COND_PAD = 16

V7X_VMEM_LIMIT_BYTES = 56 * 1024 * 1024

ROW_TILE = 256
MOE_TM = 256
MOE_TF = 512
MOE_TN = 2048
MOE_NBLK = N_TOK * TOP_K // MOE_TM + N_EXPERTS
MOE_SLOTS = MOE_NBLK * MOE_TM


def _cond_row_of_tile(tile, rows_per_tile):
    row0 = tile * rows_per_tile
    return jnp.where(row0 < N_PROMPT, 0, 1 + (row0 - N_PROMPT) // DEC_SEQ)


def _params(semantics):
    return pltpu.CompilerParams(dimension_semantics=semantics, vmem_limit_bytes=V7X_VMEM_LIMIT_BYTES)


def _mm_kernel(*refs, has_bias, has_res):
    a_ref, w_ref = refs[0], refs[1]
    pos = 2
    bias_ref = res_ref = gate_ref = None
    if has_bias:
        bias_ref = refs[pos]
        pos += 1
    if has_res:
        res_ref, gate_ref = refs[pos], refs[pos + 1]
        pos += 2
    o_ref, wbf_ref = refs[pos], refs[pos + 1]

    @pl.when(pl.program_id(1) == 0)
    def _():
        wbf_ref[...] = w_ref[...].astype(BF16)

    acc = jnp.dot(a_ref[...], wbf_ref[...], preferred_element_type=F32)
    if has_bias:
        acc = acc + bias_ref[...]
    if has_res:
        acc = res_ref[...] + gate_ref[...] * acc
    o_ref[...] = acc.astype(o_ref.dtype)


def matmul(a, w, prefix=(), *, n_cols=None, bias=None, residual=None, gate=None, out_dtype=F32, tm=1024, tn=512,
           name="mm"):
    m, k = a.shape
    n = w.shape[-1] if n_cols is None else n_cols
    assert w.shape[-2] == k and len(w.shape) == len(prefix) + 2
    tm = min(tm, m)
    tn = min(tn, n)
    assert m % tm == 0 and n % tn == 0
    grid = (n // tn, m // tm)
    npre = len(prefix)
    in_specs = [
        pl.BlockSpec((tm, k), lambda j, i: (i, 0)),
        pl.BlockSpec((None,) * npre + (k, tn), lambda j, i: prefix + (0, j)),
    ]
    args = [a, w]
    if bias is not None:
        b2 = bias.reshape(bias.shape[:-1] + (1, bias.shape[-1]))
        in_specs.append(pl.BlockSpec((None,) * npre + (1, tn), lambda j, i: prefix + (0, j)))
        args.append(b2)
    if residual is not None:
        in_specs.append(pl.BlockSpec((tm, tn), lambda j, i: (i, j)))
        in_specs.append(pl.BlockSpec((None, 1, tn), lambda j, i: (_cond_row_of_tile(i, tm), 0, j)))
        args += [residual, gate]
    return pl.pallas_call(
        functools.partial(_mm_kernel, has_bias=bias is not None, has_res=residual is not None),
        out_shape=jax.ShapeDtypeStruct((m, n), out_dtype),
        grid=grid,
        in_specs=in_specs,
        out_specs=pl.BlockSpec((tm, tn), lambda j, i: (i, j)),
        scratch_shapes=[pltpu.VMEM((k, tn), BF16)],
        compiler_params=_params(("arbitrary", "arbitrary")),
        name=name,
    )(*args)


def _mod_kernel(a_ref, w_ref, b_ref, o_ref):
    acc = jnp.dot(a_ref[...], w_ref[...].astype(BF16), preferred_element_type=F32)
    o_ref[...] = acc + b_ref[...]


def modulation(cond_act, w_mod, b_mod, tn=512):
    depth, k, n = w_mod.shape
    return pl.pallas_call(
        _mod_kernel,
        out_shape=jax.ShapeDtypeStruct((depth, N_COND_PAD, n), F32),
        grid=(depth, n // tn),
        in_specs=[
            pl.BlockSpec((N_COND_PAD, k), lambda l, j: (0, 0)),
            pl.BlockSpec((None, k, tn), lambda l, j: (l, 0, j)),
            pl.BlockSpec((None, 1, tn), lambda l, j: (l, 0, j)),
        ],
        out_specs=pl.BlockSpec((None, N_COND_PAD, tn), lambda l, j: (l, 0, j)),
        compiler_params=_params(("arbitrary", "arbitrary")),
        name="adaln_modulation",
    )(cond_act, w_mod, b_mod.reshape(depth, 1, n))


def _norm_kernel(*refs, modulate, router):
    x_ref, nw_ref = refs[0], refs[1]
    pos = 2
    if modulate:
        sc_ref, sh_ref = refs[pos], refs[pos + 1]
        pos += 2
    if router:
        rw_ref, rb_ref = refs[pos], refs[pos + 1]
        pos += 2
    h_ref = refs[pos]
    x = x_ref[...]
    xn = x * lax.rsqrt(jnp.mean(x * x, axis=-1, keepdims=True) + NORM_EPS)
    h = xn * nw_ref[...]
    if modulate:
        h = h * (1.0 + sc_ref[...]) + sh_ref[...]
    h_ref[...] = h.astype(h_ref.dtype)
    if router:
        logit_ref = refs[pos + 1]
        logit_ref[...] = jnp.dot(h.astype(BF16), rw_ref[...].astype(BF16), preferred_element_type=F32) + rb_ref[...]


def norm_modulate(x, norm_w, layer, scale=None, shift=None, router_w=None, router_b=None, out_dtype=BF16):
    n, d = x.shape
    tm = ROW_TILE
    modulate = scale is not None
    router = router_w is not None
    in_specs = [pl.BlockSpec((tm, d), lambda i: (i, 0)),
                pl.BlockSpec((None, 1, d), lambda i: (layer, 0, 0))]
    args = [x, norm_w.reshape(norm_w.shape[0], 1, d)]
    if modulate:
        spec = pl.BlockSpec((None, 1, d), lambda i: (_cond_row_of_tile(i, tm), 0, 0))
        in_specs += [spec, spec]
        args += [scale, shift]
    out_shape = [jax.ShapeDtypeStruct((n, d), out_dtype)]
    out_specs = [pl.BlockSpec((tm, d), lambda i: (i, 0))]
    if router:
        in_specs += [pl.BlockSpec((None, d, N_EXPERTS), lambda i: (layer, 0, 0)),
                     pl.BlockSpec((None, 1, N_EXPERTS), lambda i: (layer, 0, 0))]
        args += [router_w, router_b.reshape(router_b.shape[0], 1, N_EXPERTS)]
        out_shape.append(jax.ShapeDtypeStruct((n, N_EXPERTS), F32))
        out_specs.append(pl.BlockSpec((tm, N_EXPERTS), lambda i: (i, 0)))
    res = pl.pallas_call(
        functools.partial(_norm_kernel, modulate=modulate, router=router),
        out_shape=out_shape,
        grid=(n // tm,),
        in_specs=in_specs,
        out_specs=out_specs,
        compiler_params=_params(("arbitrary",)),
        name="norm_modulate",
    )(*args)
    return res if router else res[0]


def _softmax_pv(s, v):
    m = jnp.max(s, axis=-1, keepdims=True)
    p = jnp.exp(s - m)
    p = p * (1.0 / jnp.sum(p, axis=-1, keepdims=True))
    return jnp.dot(p.astype(BF16), v, preferred_element_type=F32)


_NT = (((1,), (1,)), ((), ()))


def _gqa_kernel(q_ref, k_ref, v_ref, o_ref, *, scale):
    k = k_ref[...]
    v = v_ref[...]
    for h in range(GQA_GROUP):
        cols = slice(h * HEAD_DIM, (h + 1) * HEAD_DIM)
        s = lax.dot_general(q_ref[:, cols], k, _NT, preferred_element_type=F32) * scale
        o_ref[:, cols] = _softmax_pv(s, v).astype(o_ref.dtype)


def gqa_attention(q, k, v, tq=256):
    b, s, _ = q.shape
    sk = k.shape[1]
    gw = GQA_GROUP * HEAD_DIM
    return pl.pallas_call(
        functools.partial(_gqa_kernel, scale=HEAD_DIM ** -0.5),
        out_shape=jax.ShapeDtypeStruct(q.shape, BF16),
        grid=(b, GQA_KV_HEADS, s // tq),
        in_specs=[
            pl.BlockSpec((None, tq, gw), lambda bi, j, qi: (bi, qi, j)),
            pl.BlockSpec((None, sk, HEAD_DIM), lambda bi, j, qi: (bi, 0, j)),
            pl.BlockSpec((None, sk, HEAD_DIM), lambda bi, j, qi: (bi, 0, j)),
        ],
        out_specs=pl.BlockSpec((None, tq, gw), lambda bi, j, qi: (bi, qi, j)),
        compiler_params=_params(("arbitrary", "arbitrary", "arbitrary")),
        name="gqa_attention",
    )(q, k, v)


MLA_HEADS_PER_STEP = 2


def _mla_kernel(qn_ref, qp_ref, kv_ref, kpe_ref, o_ref, *, scale):
    kpe = kpe_ref[...]
    for h in range(MLA_HEADS_PER_STEP):
        kcols = slice(h * (MLA_NOPE + MLA_V), h * (MLA_NOPE + MLA_V) + MLA_NOPE)
        vcols = slice(h * (MLA_NOPE + MLA_V) + MLA_NOPE, (h + 1) * (MLA_NOPE + MLA_V))
        s = lax.dot_general(qn_ref[:, h * MLA_NOPE:(h + 1) * MLA_NOPE], kv_ref[:, kcols], _NT,
                            preferred_element_type=F32)
        s = s + lax.dot_general(qp_ref[:, h * MLA_ROPE:(h + 1) * MLA_ROPE], kpe, _NT,
                                preferred_element_type=F32)
        o_ref[:, h * MLA_V:(h + 1) * MLA_V] = _softmax_pv(s * scale, kv_ref[:, vcols]).astype(o_ref.dtype)


def mla_attention(q_nope, q_pe, kv, k_pe, tq=256):
    b, s, _ = q_nope.shape
    sk = kv.shape[1]
    hp = MLA_HEADS_PER_STEP
    return pl.pallas_call(
        functools.partial(_mla_kernel, scale=(MLA_NOPE + MLA_ROPE) ** -0.5),
        out_shape=jax.ShapeDtypeStruct((b, s, MLA_HEADS * MLA_V), BF16),
        grid=(b, MLA_HEADS // hp, s // tq),
        in_specs=[
            pl.BlockSpec((None, tq, hp * MLA_NOPE), lambda bi, g, qi: (bi, qi, g)),
            pl.BlockSpec((None, tq, hp * MLA_ROPE), lambda bi, g, qi: (bi, qi, g)),
            pl.BlockSpec((None, sk, hp * (MLA_NOPE + MLA_V)), lambda bi, g, qi: (bi, 0, g)),
            pl.BlockSpec((None, sk, MLA_ROPE), lambda bi, g, qi: (bi, 0, 0)),
        ],
        out_specs=pl.BlockSpec((None, tq, hp * MLA_V), lambda bi, g, qi: (bi, qi, g)),
        compiler_params=_params(("arbitrary", "arbitrary", "arbitrary")),
        name="mla_attention",
    )(q_nope, q_pe, kv, k_pe)


GDN_VREG_ROWS = 8


GDN_SOLVE_BLOCK = 16
GDN_HEADS_PER_STEP = 4


def _solve_unit_triangular(problems):
    n_rows = problems[0][0].shape[0]
    vr = GDN_VREG_ROWS
    n_groups = n_rows // vr
    gpb = GDN_SOLVE_BLOCK // vr
    n_blocks = n_rows // GDN_SOLVE_BLOCK
    all_groups = [[rhs[g * vr:(g + 1) * vr, :] for g in range(n_groups)] for _, rhs, _ in problems]
    for bi in range(n_blocks):
        for t in range(GDN_SOLVE_BLOCK):
            for (a, _, reverse), groups in zip(problems, all_groups):
                b = n_blocks - 1 - bi if reverse else bi
                g_lo, g_hi = b * gpb, (b + 1) * gpb
                j = g_hi * vr - 1 - t if reverse else g_lo * vr + t
                gj, sj = divmod(j, vr)
                xj = groups[gj][sj:sj + 1, :]
                for g in (range(gj, g_lo - 1, -1) if reverse else range(gj, g_hi)):
                    groups[g] = groups[g] - a[g * vr:(g + 1) * vr, j:j + 1] * xj
        if bi == n_blocks - 1:
            break
        for (a, _, reverse), groups in zip(problems, all_groups):
            b = n_blocks - 1 - bi if reverse else bi
            g_lo, g_hi = b * gpb, (b + 1) * gpb
            rest = range(0, g_lo) if reverse else range(g_hi, n_groups)
            zero = jnp.zeros_like(groups[0])
            x_blk = jnp.concatenate([groups[g] if g_lo <= g < g_hi else zero for g in range(n_groups)], axis=0)
            upd = jnp.dot(a[rest[0] * vr:(rest[-1] + 1) * vr, :], x_blk, precision=lax.Precision.HIGHEST,
                          preferred_element_type=F32)
            for t, g in enumerate(rest):
                groups[g] = groups[g] - upd[t * vr:(t + 1) * vr, :]
    return [jnp.concatenate(groups, axis=0) for groups in all_groups]


def _gdn_chunks(chunks):
    L = chunks[0][0].shape[0]
    ii = lax.broadcasted_iota(jnp.int32, (L, L), 0)
    jj = lax.broadcasted_iota(jnp.int32, (L, L), 1)
    pre = []
    for q, k, v, gc, gc_row, gc_end, beta, state, reverse in chunks:
        keep = (ii <= jj) if reverse else (ii >= jj)
        strict = (ii < jj) if reverse else (ii > jj)
        gamma = jnp.where(keep, jnp.exp(jnp.where(keep, gc - gc_row, 0.0)), 0.0)
        kb = k.astype(BF16)
        kk = lax.dot_general(kb, kb, _NT, preferred_element_type=F32)
        qk = lax.dot_general(q.astype(BF16), kb, _NT, preferred_element_type=F32)
        a = jnp.where(strict, kk * gamma * beta, 0.0)
        eg = jnp.exp(gc)
        pre.append((a, jnp.concatenate([v * beta, k * (beta * eg)], axis=1), (qk * gamma).astype(BF16), eg))
    xs = _solve_unit_triangular([(a, rhs, c[8]) for (a, rhs, _, _), c in zip(pre, chunks)])
    out = []
    for x, (_, _, qkg, eg), (q, k, v, gc, gc_row, gc_end, beta, state, reverse) in zip(xs, pre, chunks):
        dv = v.shape[1]
        u, w = x[:, :dv], x[:, dv:]
        ws = jnp.dot(jnp.concatenate([w, q * eg], axis=0).astype(BF16), state.astype(BF16),
                     preferred_element_type=F32)
        v_new = (u - ws[:L]).astype(BF16)
        o = ws[L:] + jnp.dot(qkg, v_new, preferred_element_type=F32)
        k_dec = (k * jnp.exp(gc_end - gc)).astype(BF16)
        new_state = state * jnp.exp(gc_end) + lax.dot_general(k_dec, v_new, (((0,), (0,)), ((), ())),
                                                              preferred_element_type=F32)
        out.append((o, new_state))
    return out


def _gdn_kernel(q_ref, k_ref, v_ref, z_ref, gcol_ref, grow_ref, s0_ref, nw_ref, o_ref, s_out_ref,
                of_scr, ob_scr, st_scr, *, n_chunks):
    L = GDN_CHUNK
    st_scr[...] = s0_ref[...]
    ii = lax.broadcasted_iota(jnp.int32, (L, L), 0)
    jj = lax.broadcasted_iota(jnp.int32, (L, L), 1)
    lower = (ii >= jj).astype(F32)
    upper = (ii <= jj).astype(F32)
    hi = lax.Precision.HIGHEST

    def chunk_step(c, carry):
        chunks, where = [], []
        for hh in range(GDN_HEADS_PER_STEP):
            cols = slice(hh * HEAD_DIM, (hh + 1) * HEAD_DIM)
            for d, reverse in ((0, False), (1, True)):
                cc = (n_chunks - 1 - c) if reverse else c
                r0 = pl.multiple_of(cc * L, L)
                q = q_ref[pl.ds(r0, L), cols]
                k = k_ref[pl.ds(r0, L), cols]
                v = v_ref[pl.ds(r0, L), cols]
                q = q * lax.rsqrt(jnp.sum(q * q, axis=-1, keepdims=True) + NORM_EPS) * (HEAD_DIM ** -0.5)
                k = k * lax.rsqrt(jnp.sum(k * k, axis=-1, keepdims=True) + NORM_EPS)
                gcol = gcol_ref[hh, cc]
                grow = grow_ref[hh, cc]
                csum = jnp.dot(upper if reverse else lower, gcol, precision=hi, preferred_element_type=F32)
                rsum = jnp.dot(grow, lower if reverse else upper, precision=hi, preferred_element_type=F32)
                gc = csum[:, d:d + 1]
                gc_end = gc[0:1, :] if reverse else gc[L - 1:L, :]
                chunks.append((q, k, v, gc, rsum[d:d + 1, :], gc_end, gcol[:, 2 + d:3 + d], st_scr[d, hh], reverse))
                where.append((d, hh, r0, cols))
        for (o, new_state), (d, hh, r0, cols) in zip(_gdn_chunks(chunks), where):
            st_scr[d, hh] = new_state
            (ob_scr if d else of_scr)[pl.ds(r0, L), cols] = o
        return carry

    lax.fori_loop(0, n_chunks, chunk_step, 0)
    s_out_ref[...] = st_scr[...]
    for hh in range(GDN_HEADS_PER_STEP):
        cols = slice(hh * HEAD_DIM, (hh + 1) * HEAD_DIM)
        o = of_scr[:, cols] + ob_scr[:, cols]
        o = o * lax.rsqrt(jnp.mean(o * o, axis=-1, keepdims=True) + NORM_EPS) * nw_ref[...]
        z = z_ref[:, cols]
        o_ref[:, cols] = (o * (z * (1.0 / (1.0 + jnp.exp(-z))))).astype(o_ref.dtype)


def gdn_mixer(qkv, proj, z_col0, g, beta, s0, norm_w, n_batch, seq, row0):
    L = GDN_CHUNK
    nc = seq // L
    gb = jnp.concatenate([g, beta], axis=2)
    gcol = jnp.transpose(gb, (0, 3, 1, 2)).reshape(n_batch, GDN_HEADS, nc, L, 4)
    grow = jnp.transpose(g, (0, 3, 2, 1)).reshape(n_batch, GDN_HEADS, 2, nc, L)
    grow = jnp.pad(jnp.swapaxes(grow, 2, 3), ((0, 0), (0, 0), (0, 0), (0, GDN_VREG_ROWS - 2), (0, 0)))
    rb = row0 // seq
    hps = GDN_HEADS_PER_STEP
    bw = hps * HEAD_DIM
    n_hb = GDN_HEADS // hps
    assert row0 % seq == 0 and z_col0 % bw == 0
    zb = z_col0 // bw
    o, s_out = pl.pallas_call(
        functools.partial(_gdn_kernel, n_chunks=nc),
        out_shape=[jax.ShapeDtypeStruct((n_batch * seq, GDN_WIDTH), BF16),
                   jax.ShapeDtypeStruct((n_batch, 2, GDN_HEADS, HEAD_DIM, HEAD_DIM), F32)],
        grid=(n_batch, n_hb),
        in_specs=[
            pl.BlockSpec((seq, bw), lambda b, h: (rb + b, h)),
            pl.BlockSpec((seq, bw), lambda b, h: (rb + b, n_hb + h)),
            pl.BlockSpec((seq, bw), lambda b, h: (rb + b, 2 * n_hb + h)),
            pl.BlockSpec((seq, bw), lambda b, h: (rb + b, zb + h)),
            pl.BlockSpec((None, hps, nc, L, 4), lambda b, h: (b, h, 0, 0, 0)),
            pl.BlockSpec((None, hps, nc, GDN_VREG_ROWS, L), lambda b, h: (b, h, 0, 0, 0)),
            pl.BlockSpec((None, 2, hps, HEAD_DIM, HEAD_DIM), lambda b, h: (b, 0, h, 0, 0)),
            pl.BlockSpec((1, HEAD_DIM), lambda b, h: (0, 0)),
        ],
        out_specs=[
            pl.BlockSpec((seq, bw), lambda b, h: (b, h)),
            pl.BlockSpec((None, 2, hps, HEAD_DIM, HEAD_DIM), lambda b, h: (b, 0, h, 0, 0)),
        ],
        scratch_shapes=[pltpu.VMEM((seq, bw), F32), pltpu.VMEM((seq, bw), F32),
                        pltpu.VMEM((2, hps, HEAD_DIM, HEAD_DIM), F32)],
        compiler_params=_params(("arbitrary", "arbitrary")),
        name="gdn_mixer",
    )(qkv, qkv, qkv, proj, gcol, grow, s0, norm_w.reshape(1, HEAD_DIM))
    return o, s_out


SSD_HEADS_PER_GROUP = SSD_HEADS // SSD_GROUPS
SSD_GROUP_WIDTH = SSD_HEADS_PER_GROUP * SSD_HEAD_DIM


def _ssd_head_chunks(problems):
    L = problems[0][0].shape[0]
    ii = lax.broadcasted_iota(jnp.int32, (L, L), 0)
    jj = lax.broadcasted_iota(jnp.int32, (L, L), 1)
    tn = (((0,), (0,)), ((), ()))
    y_intra, y_inter, upd = [], [], []
    for xdt, cb, bm, cm, cs, cs_row, cs_end, state, reverse in problems:
        keep = (ii <= jj) if reverse else (ii >= jj)
        decay = jnp.where(keep, jnp.exp(jnp.where(keep, cs - cs_row, 0.0)), 0.0)
        y_intra.append(jnp.dot((cb * decay).astype(BF16), xdt, preferred_element_type=F32))
    for xdt, cb, bm, cm, cs, cs_row, cs_end, state, reverse in problems:
        y_inter.append(lax.dot_general((cm * jnp.exp(cs)).astype(BF16), state.astype(BF16), _NT,
                                       preferred_element_type=F32))
    for xdt, cb, bm, cm, cs, cs_row, cs_end, state, reverse in problems:
        b_dec = (bm * jnp.exp(cs_end - cs)).astype(BF16)
        upd.append(lax.dot_general(xdt, b_dec, tn, preferred_element_type=F32))
    return [(ya + yb, p[7] * jnp.exp(p[6]) + u) for ya, yb, u, p in zip(y_intra, y_inter, upd, problems)]


def _ssd_kernel(x_ref, b_ref, c_ref, dtc_ref, dac_ref, dar_ref, h0_ref, dsk_ref, y_ref, h_out_ref,
                yf_scr, yb_scr, st_scr, *, n_chunks):
    L = SSD_CHUNK
    nh = SSD_HEADS_PER_GROUP
    st_scr[...] = h0_ref[...]
    ii = lax.broadcasted_iota(jnp.int32, (L, L), 0)
    jj = lax.broadcasted_iota(jnp.int32, (L, L), 1)
    lower = (ii >= jj).astype(F32)
    upper = (ii <= jj).astype(F32)
    hi = lax.Precision.HIGHEST

    def chunk_step(c, carry):
        problems, rows = [], []
        for d, reverse in ((0, False), (1, True)):
            cc = (n_chunks - 1 - c) if reverse else c
            r0 = pl.multiple_of(cc * L, L)
            x = x_ref[pl.ds(r0, L), :]
            bm = b_ref[pl.ds(r0, L), :]
            cm = c_ref[pl.ds(r0, L), :]
            cb = lax.dot_general(cm.astype(BF16), bm.astype(BF16), _NT, preferred_element_type=F32)
            dtc = dtc_ref[cc]
            csum = jnp.dot(upper if reverse else lower, dac_ref[cc], precision=hi, preferred_element_type=F32)
            rsum = jnp.dot(dar_ref[cc], lower if reverse else upper, precision=hi, preferred_element_type=F32)
            rows.append(r0)
            for j in range(nh):
                col = d * nh + j
                xdt = (x[:, j * SSD_HEAD_DIM:(j + 1) * SSD_HEAD_DIM] * dtc[:, col:col + 1]).astype(BF16)
                cs = csum[:, col:col + 1]
                cs_end = cs[0:1, :] if reverse else cs[L - 1:L, :]
                problems.append((xdt, cb, bm, cm, cs, rsum[col:col + 1, :], cs_end, st_scr[d, j], reverse))
        results = _ssd_head_chunks(problems)
        for d in range(2):
            for j in range(nh):
                st_scr[d, j] = results[d * nh + j][1]
            ys = jnp.concatenate([results[d * nh + j][0] for j in range(nh)], axis=1)
            (yb_scr if d else yf_scr)[pl.ds(rows[d], L), :] = ys
        return carry

    lax.fori_loop(0, n_chunks, chunk_step, 0)
    h_out_ref[...] = st_scr[...]
    y_ref[...] = yf_scr[...] + yb_scr[...] + dsk_ref[...] * x_ref[...]


def ssd_mixer(xbc, dt, a, d_skip, h0, n_batch, seq, row0):
    L = SSD_CHUNK
    nc = seq // L
    nh = SSD_HEADS_PER_GROUP

    def group_cols(t):
        t = t.reshape(n_batch, seq, 2, SSD_GROUPS, nh)
        return jnp.transpose(t, (0, 3, 1, 2, 4)).reshape(n_batch, SSD_GROUPS, nc, L, 2 * nh)

    dtc = group_cols(dt)
    dac = group_cols(dt * a)
    dar = jnp.swapaxes(dac, 3, 4)
    dsk = jnp.repeat(jnp.sum(d_skip, axis=0), SSD_HEAD_DIM).reshape(1, SSD_INNER)
    rb = row0 // seq
    assert row0 % seq == 0
    gw = SSD_GROUP_WIDTH
    b_blk0 = SSD_INNER // SSD_STATE
    c_blk0 = b_blk0 + SSD_GROUPS
    y, h_out = pl.pallas_call(
        functools.partial(_ssd_kernel, n_chunks=nc),
        out_shape=[jax.ShapeDtypeStruct((n_batch * seq, SSD_INNER), F32),
                   jax.ShapeDtypeStruct((n_batch, 2, SSD_HEADS, SSD_HEAD_DIM, SSD_STATE), F32)],
        grid=(n_batch, SSD_GROUPS),
        in_specs=[
            pl.BlockSpec((seq, gw), lambda b, g: (rb + b, g)),
            pl.BlockSpec((seq, SSD_STATE), lambda b, g: (rb + b, b_blk0 + g)),
            pl.BlockSpec((seq, SSD_STATE), lambda b, g: (rb + b, c_blk0 + g)),
            pl.BlockSpec((None, None, nc, L, 2 * nh), lambda b, g: (b, g, 0, 0, 0)),
            pl.BlockSpec((None, None, nc, L, 2 * nh), lambda b, g: (b, g, 0, 0, 0)),
            pl.BlockSpec((None, None, nc, 2 * nh, L), lambda b, g: (b, g, 0, 0, 0)),
            pl.BlockSpec((None, 2, nh, SSD_HEAD_DIM, SSD_STATE), lambda b, g: (b, 0, g, 0, 0)),
            pl.BlockSpec((1, gw), lambda b, g: (0, g)),
        ],
        out_specs=[
            pl.BlockSpec((seq, gw), lambda b, g: (b, g)),
            pl.BlockSpec((None, 2, nh, SSD_HEAD_DIM, SSD_STATE), lambda b, g: (b, 0, g, 0, 0)),
        ],
        scratch_shapes=[pltpu.VMEM((seq, gw), F32), pltpu.VMEM((seq, gw), F32),
                        pltpu.VMEM((2, nh, SSD_HEAD_DIM, SSD_STATE), F32)],
        compiler_params=_params(("arbitrary", "arbitrary")),
        name="ssd_mixer",
    )(xbc, xbc, xbc, dtc, dac, dar, h0, dsk)
    return y, h_out


def _gated_norm_kernel(y_ref, z0_ref, z1_ref, w_ref, o_ref):
    z = jnp.concatenate([z0_ref[...], z1_ref[...]], axis=1)
    t = y_ref[...] * (z * (1.0 / (1.0 + jnp.exp(-z))))
    t = t * lax.rsqrt(jnp.mean(t * t, axis=-1, keepdims=True) + NORM_EPS) * w_ref[...]
    o_ref[...] = t.astype(o_ref.dtype)


def ssd_gated_norm(y, proj, z_col0, norm_w):
    n, d = y.shape
    tm = ROW_TILE
    half = d // 2
    assert z_col0 % half == 0
    zb = z_col0 // half
    return pl.pallas_call(
        _gated_norm_kernel,
        out_shape=jax.ShapeDtypeStruct((n, d), BF16),
        grid=(n // tm,),
        in_specs=[pl.BlockSpec((tm, d), lambda i: (i, 0)),
                  pl.BlockSpec((tm, half), lambda i: (i, zb)),
                  pl.BlockSpec((tm, half), lambda i: (i, zb + 1)),
                  pl.BlockSpec((1, d), lambda i: (0, 0))],
        out_specs=pl.BlockSpec((tm, d), lambda i: (i, 0)),
        compiler_params=_params(("arbitrary",)),
        name="ssd_gated_norm",
    )(y, proj, proj, norm_w.reshape(1, d))


ITEM_VALID = 1
ITEM_NEW_WEIGHTS = 2


def _moe_up_kernel(ib_ref, ic_ref, iwe_ref, iwc_ref, iflag_ref, x_ref, wg_ref, wu_ref, bg_ref, bu_ref, o_ref,
                   wg_bf, wu_bf):
    flag = iflag_ref[pl.program_id(0)]

    @pl.when((flag & ITEM_NEW_WEIGHTS) != 0)
    def _():
        wg_bf[...] = wg_ref[...].astype(BF16)
        wu_bf[...] = wu_ref[...].astype(BF16)

    @pl.when((flag & ITEM_VALID) != 0)
    def _():
        x = x_ref[...]
        gate = jnp.minimum(jnp.dot(x, wg_bf[...], preferred_element_type=F32) + bg_ref[...], SWIGLU_LIMIT)
        up = jnp.clip(jnp.dot(x, wu_bf[...], preferred_element_type=F32) + bu_ref[...], -SWIGLU_LIMIT, SWIGLU_LIMIT)
        act = (up + 1.0) * gate * (1.0 / (1.0 + jnp.exp(-SWIGLU_ALPHA * gate)))
        o_ref[...] = act.astype(o_ref.dtype)

    @pl.when((flag & ITEM_VALID) == 0)
    def _():
        o_ref[...] = jnp.zeros_like(o_ref)


def _moe_down_kernel(ib_ref, ic_ref, iwe_ref, iwc_ref, iflag_ref, a_ref, wd_ref, bd_ref, o_ref, wd_bf):
    flag = iflag_ref[pl.program_id(0)]

    @pl.when((flag & ITEM_NEW_WEIGHTS) != 0)
    def _():
        wd_bf[...] = wd_ref[...].astype(BF16)

    @pl.when((flag & ITEM_VALID) != 0)
    def _():
        o_ref[...] = jnp.dot(a_ref[...], wd_bf[...], preferred_element_type=F32) + bd_ref[...]

    @pl.when((flag & ITEM_VALID) == 0)
    def _():
        o_ref[...] = jnp.zeros_like(o_ref)


def _moe_items(blk_start, nblk, n_used, n_chunks):
    n_items = MOE_NBLK * n_chunks
    i = jnp.arange(n_items, dtype=jnp.int32)
    item_end = (blk_start + nblk) * n_chunks
    e = jnp.minimum(jnp.sum((item_end[None, :] <= i[:, None]).astype(jnp.int32), axis=1), N_EXPERTS - 1)
    r = i - blk_start[e] * n_chunks
    nb = jnp.maximum(nblk[e], 1)
    valid = i < n_used * n_chunks
    n_tail = jnp.maximum(MOE_NBLK - n_used, 1)
    rt = i - n_used * n_chunks
    chunk = jnp.where(valid, r // nb, rt // n_tail).astype(jnp.int32)
    blk = jnp.where(valid, blk_start[e] + r % nb, n_used + rt % n_tail).astype(jnp.int32)
    last = jnp.maximum(n_used * n_chunks - 1, 0)
    we = jnp.where(valid, e, e[last]).astype(jnp.int32)
    wc = jnp.where(valid, chunk, chunk[last]).astype(jnp.int32)
    new_w = jnp.concatenate([jnp.ones((1,), bool), (we[1:] != we[:-1]) | (wc[1:] != wc[:-1])])
    flag = valid.astype(jnp.int32) * ITEM_VALID + new_w.astype(jnp.int32) * ITEM_NEW_WEIGHTS
    return blk, chunk, we, wc, flag


MOE_COMBINE_TT = 128


MOE_COMBINE_UNROLL = 8
MOE_COMBINE_ROW_GROUP = 8


def _combine_copy(y_hbm, buf, sem, slot, k, src_row, t, n_rows):
    return pltpu.make_async_copy(y_hbm.at[pl.ds(src_row, n_rows), :], buf.at[slot, k, pl.ds(t, n_rows), :],
                                 sem.at[slot])


def _moe_combine_kernel(pos_ref, pos_next_ref, y_hbm, rw_ref, x_ref, gate_ref, o_ref, buf, sem):
    i = pl.program_id(0)
    n = pl.num_programs(0)
    tt = MOE_COMBINE_TT

    def start_all(p_ref, slot):
        for k in range(TOP_K):
            def body(t, carry):
                _combine_copy(y_hbm, buf, sem, slot, k, p_ref[k, t], t, 1).start()
                return carry
            lax.fori_loop(0, tt, body, 0, unroll=MOE_COMBINE_UNROLL)

    @pl.when(i == 0)
    def _():
        start_all(pos_ref, 0)

    @pl.when(i + 1 < n)
    def _():
        start_all(pos_next_ref, (i + 1) % 2)

    slot = i % 2
    for k in range(TOP_K):
        _combine_copy(y_hbm, buf, sem, slot, k, 0, 0, tt).wait()

    rg = MOE_COMBINE_ROW_GROUP

    def sum_rows(g, carry):
        rows = pl.ds(pl.multiple_of(g * rg, rg), rg)
        rw = rw_ref[rows, :]
        ffn = buf[slot, 0, rows, :] * rw[:, 0:1]
        for k in range(1, TOP_K):
            ffn = ffn + buf[slot, k, rows, :] * rw[:, k:k + 1]
        o_ref[rows, :] = x_ref[rows, :] + gate_ref[...] * ffn
        return carry
    lax.fori_loop(0, tt // rg, sum_rows, 0)


def moe_combine(y, slot_of_assign, router_weights, x, gate):
    t, d = x.shape
    tt = MOE_COMBINE_TT
    n_steps = t // tt
    pos = jnp.swapaxes(slot_of_assign.reshape(n_steps, tt, TOP_K), 1, 2)
    smem_spec = functools.partial(pl.BlockSpec, (None, TOP_K, tt), memory_space=pltpu.SMEM)
    return pl.pallas_call(
        _moe_combine_kernel,
        out_shape=jax.ShapeDtypeStruct((t, d), F32),
        grid=(n_steps,),
        in_specs=[
            smem_spec(lambda i: (i, 0, 0)),
            smem_spec(lambda i: (jnp.minimum(i + 1, n_steps - 1), 0, 0)),
            pl.BlockSpec(memory_space=pl.ANY),
            pl.BlockSpec((tt, TOP_K), lambda i: (i, 0)),
            pl.BlockSpec((tt, d), lambda i: (i, 0)),
            pl.BlockSpec((None, 1, d), lambda i: (_cond_row_of_tile(i, tt), 0, 0)),
        ],
        out_specs=pl.BlockSpec((tt, d), lambda i: (i, 0)),
        scratch_shapes=[pltpu.VMEM((2, TOP_K, tt, d), F32), pltpu.SemaphoreType.DMA((2,))],
        compiler_params=_params(("arbitrary",)),
        name="moe_combine",
    )(pos, pos, y, router_weights, x, gate)


def moe_ffn(h, logits, layer, w_gate, b_gate, w_up, b_up, w_down, b_down, x, gate):
    t, d = h.shape
    n_assign = t * TOP_K
    top_logit, top_idx = lax.top_k(logits, TOP_K)
    gates = jax.nn.softmax(top_logit, axis=-1)
    flat_e = top_idx.reshape(n_assign).astype(jnp.int32)
    rb = 128
    one_hot = (flat_e[:, None] == jnp.arange(N_EXPERTS, dtype=jnp.int32)).astype(F32).reshape(n_assign // rb, rb,
                                                                                              N_EXPERTS)
    within = jnp.einsum('ij,bjk->bik', jnp.tril(jnp.ones((rb, rb), F32)), one_hot)
    block_tot = within[:, -1, :]
    before = jnp.cumsum(block_tot, axis=0) - block_tot
    rank = (jnp.sum(one_hot * (within + before[:, None, :]), axis=-1) - 1.0).reshape(n_assign).astype(jnp.int32)
    counts = jnp.sum(block_tot, axis=0).astype(jnp.int32)
    nblk = (counts + MOE_TM - 1) // MOE_TM
    blk_end = jnp.cumsum(nblk).astype(jnp.int32)
    blk_start = blk_end - nblk
    n_used = blk_end[-1]
    sort_start = jnp.cumsum(counts).astype(jnp.int32) - counts
    slot_of_assign = (blk_start[flat_e] * MOE_TM + rank).reshape(t, TOP_K)
    order = jnp.argsort(flat_e).astype(jnp.int32)
    slot = jnp.arange(MOE_SLOTS, dtype=jnp.int32)
    blk = jnp.arange(MOE_NBLK, dtype=jnp.int32)
    blk_e = jnp.minimum(jnp.sum((blk_end[None, :] <= blk[:, None]).astype(jnp.int32), axis=1), N_EXPERTS - 1)
    slot_e = jnp.repeat(blk_e, MOE_TM)
    off = slot - blk_start[slot_e] * MOE_TM
    valid = (slot // MOE_TM < n_used) & (off < counts[slot_e])
    slot_tok = jnp.where(valid, order[jnp.clip(sort_start[slot_e] + off, 0, n_assign - 1)] // TOP_K, t)
    x_slots = jnp.concatenate([h, jnp.zeros((1, d), h.dtype)], axis=0)[slot_tok]

    depth = w_gate.shape[0]
    n_fc = EXPERT_FF // MOE_TF
    items = _moe_items(blk_start, nblk, n_used, n_fc)
    act = pl.pallas_call(
        _moe_up_kernel,
        out_shape=jax.ShapeDtypeStruct((MOE_SLOTS, EXPERT_FF), BF16),
        grid_spec=pltpu.PrefetchScalarGridSpec(
            num_scalar_prefetch=5,
            grid=(MOE_NBLK * n_fc,),
            in_specs=[
                pl.BlockSpec((MOE_TM, d), lambda i, ib, ic, iwe, iwc, ifl: (ib[i], 0)),
                pl.BlockSpec((None, None, d, MOE_TF), lambda i, ib, ic, iwe, iwc, ifl: (layer, iwe[i], 0, iwc[i])),
                pl.BlockSpec((None, None, d, MOE_TF), lambda i, ib, ic, iwe, iwc, ifl: (layer, iwe[i], 0, iwc[i])),
                pl.BlockSpec((None, None, 1, MOE_TF), lambda i, ib, ic, iwe, iwc, ifl: (layer, iwe[i], 0, iwc[i])),
                pl.BlockSpec((None, None, 1, MOE_TF), lambda i, ib, ic, iwe, iwc, ifl: (layer, iwe[i], 0, iwc[i])),
            ],
            out_specs=pl.BlockSpec((MOE_TM, MOE_TF), lambda i, ib, ic, iwe, iwc, ifl: (ib[i], ic[i])),
            scratch_shapes=[pltpu.VMEM((d, MOE_TF), BF16), pltpu.VMEM((d, MOE_TF), BF16)],
        ),
        compiler_params=_params(("arbitrary",)),
        name="moe_gate_up",
    )(*items, x_slots, w_gate, w_up,
      b_gate.reshape(depth, N_EXPERTS, 1, EXPERT_FF), b_up.reshape(depth, N_EXPERTS, 1, EXPERT_FF))

    n_nc = d // MOE_TN
    items = _moe_items(blk_start, nblk, n_used, n_nc)
    y = pl.pallas_call(
        _moe_down_kernel,
        out_shape=jax.ShapeDtypeStruct((MOE_SLOTS, d), F32),
        grid_spec=pltpu.PrefetchScalarGridSpec(
            num_scalar_prefetch=5,
            grid=(MOE_NBLK * n_nc,),
            in_specs=[
                pl.BlockSpec((MOE_TM, EXPERT_FF), lambda i, ib, ic, iwe, iwc, ifl: (ib[i], 0)),
                pl.BlockSpec((None, None, EXPERT_FF, MOE_TN),
                             lambda i, ib, ic, iwe, iwc, ifl: (layer, iwe[i], 0, iwc[i])),
                pl.BlockSpec((None, None, 1, MOE_TN), lambda i, ib, ic, iwe, iwc, ifl: (layer, iwe[i], 0, iwc[i])),
            ],
            out_specs=pl.BlockSpec((MOE_TM, MOE_TN), lambda i, ib, ic, iwe, iwc, ifl: (ib[i], ic[i])),
            scratch_shapes=[pltpu.VMEM((EXPERT_FF, MOE_TN), BF16)],
        ),
        compiler_params=_params(("arbitrary",)),
        name="moe_down",
    )(*items, act, w_down, b_down.reshape(depth, N_EXPERTS, 1, d))
    return moe_combine(y, slot_of_assign, gates, x, gate)


def _split(x, sizes):
    return jnp.split(x, np.cumsum(sizes)[:-1].tolist(), axis=-1)


def _rms(x, w):
    return (x * lax.rsqrt(jnp.mean(x * x, axis=-1, keepdims=True) + NORM_EPS)) * w


def _rope_1d(x, pos):
    half = x.shape[-1] // 2
    inv_freq = ROPE_BASE ** (-jnp.arange(half, dtype=F32) / half)
    ang = pos.astype(F32)[:, None] * inv_freq
    cos = jnp.cos(ang)[:, None, :]
    sin = jnp.sin(ang)[:, None, :]
    x1, x2 = x[..., :half], x[..., half:]
    return jnp.concatenate([x1 * cos - x2 * sin, x2 * cos + x1 * sin], axis=-1)


def _rope_2d(x, n_rows):
    t = jnp.arange(n_rows * GRID_W)
    d_axis = x.shape[-1] // 2
    return jnp.concatenate([_rope_1d(x[..., :d_axis], t // GRID_W), _rope_1d(x[..., d_axis:], t % GRID_W)], axis=-1)


def _short_conv(x, w):
    pad = (w.shape[0] - 1) // 2
    s = x.shape[1]
    xp = jnp.pad(x, ((0, 0), (pad, pad), (0, 0)))
    out = xp[:, 0:s] * w[0]
    for j in range(1, w.shape[0]):
        out = out + xp[:, j:j + s] * w[j]
    return out


def _conv_silu_merged(cols, conv_w, conv_b=None):
    outs = []
    for t in (cols[:N_PROMPT].reshape(BATCH, SEQ, -1), cols[N_PROMPT:].reshape(DEC_BATCH, DEC_SEQ, -1)):
        t = _short_conv(t, conv_w)
        if conv_b is not None:
            t = t + conv_b
        outs.append(jax.nn.silu(t).reshape(-1, t.shape[-1]))
    return jnp.concatenate(outs, axis=0)


def _even_attention_pass(proj, q_norm, k_norm, ctx, n_rows):
    b, s, _ = proj.shape
    q, k, v = _split(proj[..., :sum(EVEN_SPLITS[:3])], EVEN_SPLITS[:3])
    q = _rms(q.reshape(b, s, GQA_HEADS, HEAD_DIM), q_norm)
    k = _rms(k.reshape(b, s, GQA_KV_HEADS, HEAD_DIM), k_norm)
    v = v.reshape(b, s, GQA_KV_HEADS, HEAD_DIM)
    if ctx is None:
        k_all, v_all = k, v
    else:
        k_ctx, v_ctx = ctx
        q = _rope_2d(q, n_rows)
        k_all = jnp.concatenate([k_ctx, _rope_2d(k, n_rows)], axis=1)
        v_all = jnp.concatenate([v_ctx, v], axis=1)
    sk = k_all.shape[1]
    attn = gqa_attention(q.reshape(b, s, -1).astype(BF16), k_all.reshape(b, sk, -1).astype(BF16),
                         v_all.reshape(b, sk, -1).astype(BF16))
    return attn, k, v


def kernel(x_prompt, x_sample, cache_gqa_k, cache_gqa_v, state_ssd, state_gdn, cache_mla_ckv, cache_mla_kpe, c, c_ctx, w_mod, b_mod, norm_mix, norm_ffn, ev_w_in, ev_w_out, gqa_q_norm, gqa_k_norm, ssd_conv_w, ssd_conv_b, ssd_a_log, ssd_dt_bias, ssd_d_skip, ssd_norm, od_w_in, od_w_out, gdn_conv_w, gdn_a_log, gdn_dt_bias, gdn_norm, mla_q_a_norm, mla_w_q_b, mla_kv_a_norm, mla_w_kv_b, router_w, router_b, moe_w_gate, moe_b_gate, moe_w_up, moe_b_up, moe_w_down, moe_b_down, final_norm):
    n_rows = DEC_SEQ // GRID_W
    x = jnp.concatenate([x_prompt.reshape(N_PROMPT, D_MODEL), x_sample.reshape(N_SAMPLE, D_MODEL)], axis=0)

    cond = jnp.concatenate([c_ctx[None, :], c, jnp.zeros((N_COND_PAD - N_COND, D_MODEL), F32)], axis=0)
    mods = modulation(jax.nn.silu(cond).astype(BF16), w_mod, b_mod)
    mods = mods.reshape(DEPTH, N_COND_PAD, N_MOD, 1, D_MODEL)

    def split_passes(t):
        return (t[:N_PROMPT].reshape(BATCH, SEQ, -1), t[N_PROMPT:].reshape(DEC_BATCH, DEC_SEQ, -1))

    def merge_passes(tp, ts):
        return jnp.concatenate([tp.reshape(N_PROMPT, -1), ts.reshape(N_SAMPLE, -1)], axis=0)

    ctx_even, ctx_odd = [], []
    for l in range(DEPTH):
        i = l // 2
        sh1, sc1, g1, sh2, sc2, g2 = (mods[l, :, j] for j in range(N_MOD))
        h = norm_modulate(x, norm_mix, l, sc1, sh1)
        if l % 2 == 0:
            proj = matmul(h, ev_w_in, (i,), n_cols=EVEN_MAIN, name="even_in_proj")
            dt_proj = matmul(h, ev_w_in[i, :, EVEN_MAIN:], name="even_dt_proj")
            pp, ps = split_passes(proj)
            attn_p, k_p, v_p = _even_attention_pass(pp, gqa_q_norm[i], gqa_k_norm[i], None, None)
            attn_s, _, _ = _even_attention_pass(ps, gqa_q_norm[i], gqa_k_norm[i],
                                                (cache_gqa_k[:, i], cache_gqa_v[:, i]), n_rows)
            xbc = _conv_silu_merged(proj[:, EVEN_MAIN - SSD_CONV_CH:], ssd_conv_w[i], ssd_conv_b[i])
            dt = jax.nn.softplus(dt_proj.reshape(N_TOK, 2, SSD_HEADS) + ssd_dt_bias[i])
            dt_p, dt_s = dt[:N_PROMPT].reshape(BATCH, SEQ, 2, SSD_HEADS), dt[N_PROMPT:].reshape(
                DEC_BATCH, DEC_SEQ, 2, SSD_HEADS)
            a = -jnp.exp(ssd_a_log[i])
            y_p, h_p = ssd_mixer(xbc, dt_p, a, ssd_d_skip[i],
                                 jnp.zeros((BATCH, 2, SSD_HEADS, SSD_HEAD_DIM, SSD_STATE), F32), BATCH, SEQ, 0)
            y_s, _ = ssd_mixer(xbc, dt_s, a, ssd_d_skip[i], state_ssd[:, i], DEC_BATCH, DEC_SEQ, N_PROMPT)
            yn = ssd_gated_norm(jnp.concatenate([y_p, y_s], axis=0), proj, sum(EVEN_SPLITS[:3]), ssd_norm[i])
            ctx_even.append((k_p, v_p, h_p))
            feat = jnp.concatenate([merge_passes(attn_p, attn_s), yn], axis=-1)
            x = matmul(feat, ev_w_out, (i,), residual=x, gate=g1, name="even_out_proj")
        else:
            proj = matmul(h, od_w_in, (i,), tm=512, tn=896, name="odd_in_proj")
            off = np.cumsum((0,) + ODD_SPLITS)
            qkv = _conv_silu_merged(proj[:, :off[1]], gdn_conv_w[i])
            gates = proj[:, off[2]:off[3]].reshape(N_TOK, 2, 2, GDN_HEADS)
            g = -jnp.exp(gdn_a_log[i]) * jax.nn.softplus(gates[:, 0] + gdn_dt_bias[i])
            beta = jax.nn.sigmoid(gates[:, 1])

            def gate_passes(t):
                return (t[:N_PROMPT].reshape(BATCH, SEQ, 2, GDN_HEADS),
                        t[N_PROMPT:].reshape(DEC_BATCH, DEC_SEQ, 2, GDN_HEADS))

            (g_p, g_s), (beta_p, beta_s) = gate_passes(g), gate_passes(beta)
            o_p, s_p = gdn_mixer(qkv, proj, int(off[1]), g_p, beta_p,
                                 jnp.zeros((BATCH, 2, GDN_HEADS, HEAD_DIM, HEAD_DIM), F32), gdn_norm[i], BATCH, SEQ, 0)
            o_s, _ = gdn_mixer(qkv, proj, int(off[1]), g_s, beta_s, state_gdn[:, i], gdn_norm[i], DEC_BATCH, DEC_SEQ,
                               N_PROMPT)
            o_p = o_p.reshape(BATCH, SEQ, -1)
            o_s = o_s.reshape(DEC_BATCH, DEC_SEQ, -1)
            qa = _rms(proj[:, off[3]:off[4]], mla_q_a_norm[i]).astype(BF16)
            ckv = _rms(proj[:, off[4]:off[5]], mla_kv_a_norm[i])
            ckv_p, ckv_s = split_passes(ckv)
            kpe_p, kpe_s = split_passes(proj[:, off[5]:off[6]])
            ctx_odd.append((s_p, ckv_p, kpe_p))
            qm = matmul(qa, mla_w_q_b, (i,), name="mla_q_b")
            ckv_all_s = jnp.concatenate([cache_mla_ckv[:, i], ckv_s], axis=1)
            sk_s = PAST_LEN + DEC_SEQ
            kv = matmul(jnp.concatenate([ckv_p.reshape(N_PROMPT, -1), ckv_all_s.reshape(DEC_BATCH * sk_s, -1)],
                                        axis=0).astype(BF16),
                        mla_w_kv_b, (i,), out_dtype=BF16, name="mla_kv_b")
            kv_p = kv[:N_PROMPT].reshape(BATCH, SEQ, -1)
            kv_s = kv[N_PROMPT:].reshape(DEC_BATCH, sk_s, -1)
            qm_p, qm_s = split_passes(qm)
            qm_p = qm_p.reshape(BATCH, SEQ, MLA_HEADS, MLA_NOPE + MLA_ROPE)
            qm_s = qm_s.reshape(DEC_BATCH, DEC_SEQ, MLA_HEADS, MLA_NOPE + MLA_ROPE)
            qpe_s = _rope_2d(qm_s[..., MLA_NOPE:], n_rows)
            kpe_all_s = jnp.concatenate([cache_mla_kpe[:, i], _rope_2d(kpe_s[:, :, None, :], n_rows)[:, :, 0]],
                                        axis=1)
            om_p = mla_attention(qm_p[..., :MLA_NOPE].reshape(BATCH, SEQ, -1).astype(BF16),
                                 qm_p[..., MLA_NOPE:].reshape(BATCH, SEQ, -1).astype(BF16), kv_p,
                                 kpe_p.astype(BF16))
            om_s = mla_attention(qm_s[..., :MLA_NOPE].reshape(DEC_BATCH, DEC_SEQ, -1).astype(BF16),
                                 qpe_s.reshape(DEC_BATCH, DEC_SEQ, -1).astype(BF16), kv_s, kpe_all_s.astype(BF16))
            feat = merge_passes(jnp.concatenate([o_p, om_p], axis=-1), jnp.concatenate([o_s, om_s], axis=-1))
            x = matmul(feat, od_w_out, (i,), residual=x, gate=g1, name="odd_out_proj")
        h, logits = norm_modulate(x, norm_ffn, l, sc2, sh2, router_w, router_b)
        x = moe_ffn(h, logits, l, moe_w_gate, moe_b_gate, moe_w_up, moe_b_up, moe_w_down, moe_b_down, x, g2)

    y = norm_modulate(x, final_norm[None, :], 0, out_dtype=F32)
    y_prompt = y[:N_PROMPT].reshape(BATCH, SEQ, D_MODEL)
    y_sample = y[N_PROMPT:].reshape(DEC_BATCH, DEC_SEQ, D_MODEL)
    new_gqa_k = jnp.stack([t[0] for t in ctx_even], axis=1)
    new_gqa_v = jnp.stack([t[1] for t in ctx_even], axis=1)
    new_ssd = jnp.stack([t[2] for t in ctx_even], axis=1)
    new_gdn = jnp.stack([t[0] for t in ctx_odd], axis=1)
    new_mla_ckv = jnp.stack([t[1] for t in ctx_odd], axis=1)
    new_mla_kpe = jnp.stack([t[2] for t in ctx_odd], axis=1)
    return (y_prompt, y_sample, new_gqa_k, new_gqa_v, new_ssd, new_gdn, new_mla_ckv, new_mla_kpe)
```
